```python
import math
import jax, jax.numpy as jnp
from jax import lax
import numpy as np

D_MODEL = 1024
BATCH = 2
SEQ = 8192
DEPTH = 4

N_HEADS = 16
N_KV_HEADS = 4
HEAD_DIM = 64
GROUP = N_HEADS // N_KV_HEADS
WINDOW = 128
BLOCK = 128
ROPE_THETA = 10000.0
D_RNN = 1024
RNN_BLOCKS = 4
RNN_BLOCK_W = D_RNN // RNN_BLOCKS
CONV_W = 4
LRU_C = 8.0
D_FF = 2816
DEEPNORM_ALPHA = (2.0 * DEPTH) ** 0.25
DEEPNORM_BETA = (8.0 * DEPTH) ** -0.25
LN_EPS = 1e-5
N_MIXERS = 2
N_ATTN_LAYERS = (DEPTH + 1) // 2
N_LRU_LAYERS = DEPTH // 2
QKV_COLS = (N_HEADS + 2 * N_KV_HEADS) * HEAD_DIM

kernel_name = "hybrid_swa_sink_rglru_macaron_deepnorm"


def _layernorm(x, g, b):
    xf = x.astype(jnp.float32)
    mu = jnp.mean(xf, axis=-1, keepdims=True)
    xc = xf - mu
    var = jnp.mean(xc * xc, axis=-1, keepdims=True)
    y = xc * lax.rsqrt(var + LN_EPS) * g.astype(jnp.float32) + b.astype(jnp.float32)
    return y.astype(x.dtype)


def _swiglu(x, w_in, w_out):
    g, u = jnp.split(x @ w_in, 2, axis=-1)
    return (jax.nn.silu(g) * u) @ w_out


def _rope(t, cos, sin):
    t1, t2 = jnp.split(t, 2, axis=-1)
    c = cos[None, :, None, :]
    s = sin[None, :, None, :]
    out = jnp.concatenate([t1 * c - t2 * s, t2 * c + t1 * s], axis=-1)
    return out.astype(t.dtype)


def _band(t):
    prev = jnp.pad(t[:, :-1], ((0, 0), (1, 0), (0, 0), (0, 0), (0, 0)))
    return jnp.concatenate([prev, t], axis=2)


def _swa_sink_attention(x, w_qkv, sinks, w_o, cos, sin):
    B, S, _ = x.shape
    nb = S // BLOCK
    qkv = x @ w_qkv
    q, k, v = jnp.split(qkv, [N_HEADS * HEAD_DIM, (N_HEADS + N_KV_HEADS) * HEAD_DIM], axis=-1)
    q = _rope(q.reshape(B, S, N_HEADS, HEAD_DIM), cos, sin)
    k = _rope(k.reshape(B, S, N_KV_HEADS, HEAD_DIM), cos, sin)
    v = v.reshape(B, S, N_KV_HEADS, HEAD_DIM)
    q = q.reshape(B, nb, BLOCK, N_KV_HEADS, GROUP, HEAD_DIM)
    kb = _band(k.reshape(B, nb, BLOCK, N_KV_HEADS, HEAD_DIM))
    vb = _band(v.reshape(B, nb, BLOCK, N_KV_HEADS, HEAD_DIM))
    s = jnp.einsum('bnqkgd,bnjkd->bnkgqj', q, kb).astype(jnp.float32) * (HEAD_DIM ** -0.5)
    qi = jnp.arange(BLOCK)[:, None]
    kj = jnp.arange(2 * BLOCK)[None, :]
    dist = qi + BLOCK - kj
    in_window = (dist >= 0) & (dist < WINDOW)
    k_pos = jnp.arange(nb)[:, None] * BLOCK - BLOCK + jnp.arange(2 * BLOCK)[None, :]
    mask = in_window[None, :, :] & (k_pos >= 0)[:, None, :]
    s = jnp.where(mask[None, :, None, None, :, :], s, jnp.finfo(jnp.float32).min)
    sink = sinks.astype(jnp.float32).reshape(N_KV_HEADS, GROUP)[None, None, :, :, None, None]
    m = jnp.maximum(jnp.max(s, axis=-1, keepdims=True), sink)
    p = jnp.exp(s - m)
    denom = jnp.sum(p, axis=-1, keepdims=True) + jnp.exp(sink - m)
    p = (p / denom).astype(x.dtype)
    o = jnp.einsum('bnkgqj,bnjkd->bnqkgd', p, vb).reshape(B, S, N_HEADS * HEAD_DIM)
    return o @ w_o


def _lru_combine(left, right):
    a1, b1 = left
    a2, b2 = right
    return a1 * a2, a2 * b1 + b2


def _rglru_block(x, w_in, conv_w, conv_b, w_ra, b_ra, w_rx, b_rx, lam, w_out):
    B, S, _ = x.shape
    xb, gb = jnp.split(x @ w_in, 2, axis=-1)
    gate = jax.nn.gelu(gb)
    xc = lax.conv_general_dilated(
        xb, conv_w[:, None, :].astype(xb.dtype), window_strides=(1,), padding=[(CONV_W - 1, 0)],
        dimension_numbers=('NWC', 'WIO', 'NWC'), feature_group_count=D_RNN) + conv_b
    xr = xc.reshape(B, S, RNN_BLOCKS, RNN_BLOCK_W)
    r = jax.nn.sigmoid(jnp.einsum('bsnc,ncd->bsnd', xr, w_ra).reshape(B, S, D_RNN) + b_ra)
    i = jax.nn.sigmoid(jnp.einsum('bsnc,ncd->bsnd', xr, w_rx).reshape(B, S, D_RNN) + b_rx)
    log_a = LRU_C * r.astype(jnp.float32) * jax.nn.log_sigmoid(lam.astype(jnp.float32))
    a = jnp.exp(log_a)
    b = jnp.sqrt(-jnp.expm1(2.0 * log_a)) * (i * xc).astype(jnp.float32)
    _, h = lax.associative_scan(_lru_combine, (a, b), axis=1)
    y = h.astype(x.dtype) * gate
    return y @ w_out


def setup_inputs(seed: int = 0) -> dict:
    key = jax.random.key(seed)
    ks = jax.random.split(key, 24)
    f32 = jnp.float32
    nrm = lambda k, shape, scale: jax.random.normal(k, shape, f32) * scale
    x = jax.random.normal(ks[0], (BATCH, SEQ, D_MODEL), f32)
    ffn1_w_in = nrm(ks[1], (DEPTH, D_MODEL, 2 * D_FF), D_MODEL ** -0.5)
    ffn1_w_out = nrm(ks[2], (DEPTH, D_FF, D_MODEL), D_FF ** -0.5 * DEEPNORM_BETA)
    ffn2_w_in = nrm(ks[3], (DEPTH, D_MODEL, 2 * D_FF), D_MODEL ** -0.5)
    ffn2_w_out = nrm(ks[4], (DEPTH, D_FF, D_MODEL), D_FF ** -0.5 * DEEPNORM_BETA)
    ln_g = 1.0 + nrm(ks[5], (DEPTH, 3, D_MODEL), 0.02)
    ln_b = nrm(ks[6], (DEPTH, 3, D_MODEL), 0.02)
    attn_w_qkv = nrm(ks[7], (N_ATTN_LAYERS, D_MODEL, QKV_COLS), D_MODEL ** -0.5)
    attn_sinks = nrm(ks[8], (N_ATTN_LAYERS, N_HEADS), 0.5)
    attn_w_o = nrm(ks[9], (N_ATTN_LAYERS, N_HEADS * HEAD_DIM, D_MODEL), (N_HEADS * HEAD_DIM) ** -0.5 * DEEPNORM_BETA)
    lru_w_in = nrm(ks[10], (N_LRU_LAYERS, D_MODEL, 2 * D_RNN), D_MODEL ** -0.5)
    lru_conv_w = nrm(ks[11], (N_LRU_LAYERS, CONV_W, D_RNN), CONV_W ** -0.5)
    lru_conv_b = nrm(ks[12], (N_LRU_LAYERS, D_RNN), 0.01)
    lru_w_ra = nrm(ks[13], (N_LRU_LAYERS, RNN_BLOCKS, RNN_BLOCK_W, RNN_BLOCK_W), RNN_BLOCK_W ** -0.5)
    lru_b_ra = nrm(ks[14], (N_LRU_LAYERS, D_RNN), 0.01)
    lru_w_rx = nrm(ks[15], (N_LRU_LAYERS, RNN_BLOCKS, RNN_BLOCK_W, RNN_BLOCK_W), RNN_BLOCK_W ** -0.5)
    lru_b_rx = nrm(ks[16], (N_LRU_LAYERS, D_RNN), 0.01)
    a_c = jax.random.uniform(ks[17], (N_LRU_LAYERS, D_RNN), f32, 0.9, 0.999)
    sig = a_c ** (1.0 / LRU_C)
    lru_lambda = jnp.log(sig) - jnp.log1p(-sig)
    lru_w_out = nrm(ks[18], (N_LRU_LAYERS, D_RNN, D_MODEL), D_RNN ** -0.5 * DEEPNORM_BETA)
    return {"x": x, "ffn1_w_in": ffn1_w_in, "ffn1_w_out": ffn1_w_out,
            "ffn2_w_in": ffn2_w_in, "ffn2_w_out": ffn2_w_out, "ln_g": ln_g, "ln_b": ln_b,
            "attn_w_qkv": attn_w_qkv, "attn_sinks": attn_sinks, "attn_w_o": attn_w_o,
            "lru_w_in": lru_w_in, "lru_conv_w": lru_conv_w, "lru_conv_b": lru_conv_b,
            "lru_w_ra": lru_w_ra, "lru_b_ra": lru_b_ra, "lru_w_rx": lru_w_rx, "lru_b_rx": lru_b_rx,
            "lru_lambda": lru_lambda, "lru_w_out": lru_w_out}


def reference(x, ffn1_w_in, ffn1_w_out, ffn2_w_in, ffn2_w_out, ln_g, ln_b,
              attn_w_qkv, attn_sinks, attn_w_o,
              lru_w_in, lru_conv_w, lru_conv_b, lru_w_ra, lru_b_ra, lru_w_rx, lru_b_rx,
              lru_lambda, lru_w_out):
    S = x.shape[1]
    pos = jnp.arange(S, dtype=jnp.float32)
    inv_freq = ROPE_THETA ** (-jnp.arange(0, HEAD_DIM, 2, dtype=jnp.float32) / HEAD_DIM)
    ang = pos[:, None] * inv_freq[None, :]
    cos, sin = jnp.cos(ang), jnp.sin(ang)
    h = x
    for i in range(DEPTH):
        h = _layernorm(DEEPNORM_ALPHA * h + 0.5 * _swiglu(h, ffn1_w_in[i], ffn1_w_out[i]), ln_g[i, 0], ln_b[i, 0])
        j = i // N_MIXERS
        if i % N_MIXERS == 0:
            mix = _swa_sink_attention(h, attn_w_qkv[j], attn_sinks[j], attn_w_o[j], cos, sin)
        else:
            mix = _rglru_block(h, lru_w_in[j], lru_conv_w[j], lru_conv_b[j], lru_w_ra[j], lru_b_ra[j],
                               lru_w_rx[j], lru_b_rx[j], lru_lambda[j], lru_w_out[j])
        h = _layernorm(DEEPNORM_ALPHA * h + mix, ln_g[i, 1], ln_b[i, 1])
        h = _layernorm(DEEPNORM_ALPHA * h + 0.5 * _swiglu(h, ffn2_w_in[i], ffn2_w_out[i]), ln_g[i, 2], ln_b[i, 2])
    return h
```

```python
import functools

import jax
import jax.numpy as jnp
from jax import lax
from jax.experimental import pallas as pl
from jax.experimental.pallas import tpu as pltpu

F32 = jnp.float32
BF16 = jnp.bfloat16

D_MODEL = 1024
DEPTH = 4
N_HEADS = 16
N_KV_HEADS = 4
HEAD_DIM = 64
GROUP = N_HEADS // N_KV_HEADS
WINDOW = 128
ROPE_THETA = 10000.0
D_RNN = 1024
RNN_BLOCKS = 4
RNN_BLOCK_W = D_RNN // RNN_BLOCKS
CONV_W = 4
LRU_C = 8.0
D_FF = 2816
ALPHA = (2.0 * DEPTH) ** 0.25
LN_EPS = 1e-5
QKV_COLS = (N_HEADS + 2 * N_KV_HEADS) * HEAD_DIM

LANES = 128
SUBLANES = 8
MIB = 1024 * 1024

TM_FFN = 512
FF_CHUNK = 256
TM_PROJ = 512
TQ_ATTN = 512
TM_LRU = 512

Q_HEAD_PERM = tuple((2 * kp + half) * GROUP + c
                    for kp in range(N_KV_HEADS // 2) for c in range(GROUP) for half in range(2))


def _resident(shape):
    nd = len(shape)
    return pl.BlockSpec(shape, lambda *_: (0,) * nd, pipeline_mode=pl.Buffered(1))


def _params(semantics, vmem_mib):
    return pltpu.CompilerParams(dimension_semantics=semantics, vmem_limit_bytes=vmem_mib * MIB)


def _layernorm(y, g, b):
    mu = jnp.mean(y, axis=-1, keepdims=True)
    yc = y - mu
    var = jnp.mean(yc * yc, axis=-1, keepdims=True)
    return yc * lax.rsqrt(var + LN_EPS) * g + b


def _ffn_kernel(x_ref, wi_ref, wo_ref, g_ref, b_ref, o_ref, acc_ref):
    x = x_ref[...]
    xb = x.astype(BF16)
    for c in range(D_FF // FF_CHUNK):
        lo = c * FF_CHUNK
        gate = jnp.dot(xb, wi_ref[:, lo:lo + FF_CHUNK], preferred_element_type=F32)
        up = jnp.dot(xb, wi_ref[:, D_FF + lo:D_FF + lo + FF_CHUNK], preferred_element_type=F32)
        act = (jax.nn.silu(gate) * up).astype(BF16)
        part = jnp.dot(act, wo_ref[lo:lo + FF_CHUNK, :], preferred_element_type=F32)
        if c == 0:
            acc_ref[...] = part
        else:
            acc_ref[...] += part
    y = ALPHA * x + 0.5 * acc_ref[...]
    o_ref[...] = _layernorm(y, g_ref[...], b_ref[...])


def _ffn_ln(x, wi, wo, g, b):
    t = x.shape[0]
    tile = pl.BlockSpec((TM_FFN, D_MODEL), lambda i: (i, 0))
    return pl.pallas_call(
        _ffn_kernel,
        grid=(t // TM_FFN,),
        in_specs=[tile, _resident((D_MODEL, 2 * D_FF)), _resident((D_FF, D_MODEL)),
                  _resident((1, D_MODEL)), _resident((1, D_MODEL))],
        out_specs=tile,
        out_shape=jax.ShapeDtypeStruct((t, D_MODEL), F32),
        scratch_shapes=[pltpu.VMEM((TM_FFN, D_MODEL), F32)],
        compiler_params=_params(("parallel",), 48),
        name="ffn_ln",
    )(x, wi, wo, g, b)


def _qkv_kernel(x_ref, w_ref, cos_ref, sa_ref, sb_ref, q_ref, k_ref, v_ref):
    z = jnp.dot(x_ref[...].astype(BF16), w_ref[...], preferred_element_type=F32)
    cos, sa, sb = cos_ref[...], sa_ref[...], sb_ref[...]

    def rope(t):
        return t * cos + pltpu.roll(t, LANES - 32, 1) * sa + pltpu.roll(t, 32, 1) * sb

    nq = N_HEADS * HEAD_DIM
    nk = N_KV_HEADS * HEAD_DIM
    for c in range(nq // LANES):
        sl = slice(c * LANES, (c + 1) * LANES)
        q_ref[:, sl] = (rope(z[:, sl]) * (HEAD_DIM ** -0.5)).astype(BF16)
    for c in range(nk // LANES):
        sl = slice(c * LANES, (c + 1) * LANES)
        k_ref[:, sl] = rope(z[:, nq + c * LANES:nq + (c + 1) * LANES]).astype(BF16)
    v_ref[...] = z[:, nq + nk:].astype(BF16)


def _qkv_rope(x, w, cos, sa, sb, seq):
    t = x.shape[0]
    nq = N_HEADS * HEAD_DIM
    nk = N_KV_HEADS * HEAD_DIM
    tiles_per_seq = seq // TM_PROJ
    tab = pl.BlockSpec((TM_PROJ, LANES), lambda i: (i % tiles_per_seq, 0))
    row = lambda n: pl.BlockSpec((TM_PROJ, n), lambda i: (i, 0))
    return pl.pallas_call(
        _qkv_kernel,
        grid=(t // TM_PROJ,),
        in_specs=[row(D_MODEL), _resident((D_MODEL, QKV_COLS)), tab, tab, tab],
        out_specs=[row(nq), row(nk), row(nk)],
        out_shape=[jax.ShapeDtypeStruct((t, nq), BF16), jax.ShapeDtypeStruct((t, nk), BF16),
                   jax.ShapeDtypeStruct((t, nk), BF16)],
        compiler_params=_params(("parallel",), 32),
        name="qkv_rope",
    )(x, w, cos, sa, sb)


def _attn_kernel(sink_ref, q_ref, kc_ref, kp_ref, vc_ref, vp_ref, o_ref, *, tiles_per_seq):
    nblk = TQ_ATTN // WINDOW
    first = (pl.program_id(0) % tiles_per_seq) == 0
    kfull = jnp.concatenate([kp_ref[...], kc_ref[...]], axis=0)
    vfull = jnp.concatenate([vp_ref[...], vc_ref[...]], axis=0)

    qi = lax.broadcasted_iota(jnp.int32, (WINDOW, 2 * WINDOW), 0)
    kj = lax.broadcasted_iota(jnp.int32, (WINDOW, 2 * WINDOW), 1)
    band = (kj > qi) & (kj <= qi + WINDOW)
    band0 = band & (kj >= jnp.where(first, WINDOW, 0))
    lane_k = lax.broadcasted_iota(jnp.int32, (2 * WINDOW, LANES), 1)
    lane_o = lax.broadcasted_iota(jnp.int32, (WINDOW, LANES), 1)
    ones = jnp.ones((2 * WINDOW, LANES), BF16)
    neg = jnp.finfo(F32).min

    for j in range(nblk):
        mask = band0 if j == 0 else band
        rows = slice(j * WINDOW, (j + 1) * WINDOW)
        keys = slice(j * WINDOW, (j + 2) * WINDOW)
        for kp in range(N_KV_HEADS // 2):
            pair = slice(kp * LANES, (kp + 1) * LANES)
            kblk = kfull[keys, pair]
            k_halves = (jnp.where(lane_k < HEAD_DIM, kblk, jnp.zeros_like(kblk)),
                        jnp.where(lane_k >= HEAD_DIM, kblk, jnp.zeros_like(kblk)))
            qstack = jnp.concatenate(
                [q_ref[rows, (kp * GROUP + c) * LANES:(kp * GROUP + c + 1) * LANES]
                 for c in range(GROUP)], axis=0)
            probs, sink_terms = [], []
            for half in range(2):
                s = lax.dot_general(qstack, k_halves[half], (((1,), (1,)), ((), ())),
                                    preferred_element_type=F32)
                for c in range(GROUP):
                    sink = sink_ref[(2 * kp + half) * GROUP + c]
                    sc = jnp.where(mask, s[c * WINDOW:(c + 1) * WINDOW, :], neg)
                    m = jnp.maximum(jnp.max(sc, axis=-1, keepdims=True), sink)
                    probs.append(jnp.exp(sc - m).astype(BF16))
                    sink_terms.append(jnp.exp(sink - m))
            p = jnp.concatenate(probs, axis=0)
            vext = jnp.concatenate([vfull[keys, pair], ones], axis=1)
            pv = jnp.dot(p, vext, preferred_element_type=F32)
            for c in range(GROUP):
                outs = []
                for half in range(2):
                    r = (half * GROUP + c) * WINDOW
                    num = pv[r:r + WINDOW, :LANES]
                    den = pv[r:r + WINDOW, LANES:] + sink_terms[half * GROUP + c]
                    outs.append(num / den)
                chunk = jnp.where(lane_o < HEAD_DIM, outs[0], outs[1])
                col = (kp * GROUP + c) * LANES
                o_ref[rows, col:col + LANES] = chunk.astype(BF16)


def _swa_attn(sinks, q, k, v, seq):
    t = q.shape[0]
    nq = N_HEADS * HEAD_DIM
    nk = N_KV_HEADS * HEAD_DIM
    blk_per_tile = TQ_ATTN // WINDOW
    cur = lambda n: pl.BlockSpec((TQ_ATTN, n), lambda i: (i, 0))
    prev = pl.BlockSpec((WINDOW, nk), lambda i: (jnp.maximum(i * blk_per_tile - 1, 0), 0))
    return pl.pallas_call(
        functools.partial(_attn_kernel, tiles_per_seq=seq // TQ_ATTN),
        grid=(t // TQ_ATTN,),
        in_specs=[pl.BlockSpec(memory_space=pltpu.SMEM), cur(nq), cur(nk), prev, cur(nk), prev],
        out_specs=cur(nq),
        out_shape=jax.ShapeDtypeStruct((t, nq), BF16),
        compiler_params=_params(("parallel",), 32),
        name="swa_attn",
    )(sinks, q, k, k, v, v)


def _proj_ln_kernel(res_ref, a_ref, w_ref, g_ref, b_ref, o_ref):
    y = ALPHA * res_ref[...] + jnp.dot(a_ref[...], w_ref[...], preferred_element_type=F32)
    o_ref[...] = _layernorm(y, g_ref[...], b_ref[...])


def _proj_ln(res, a, w, g, b):
    t = res.shape[0]
    tile = pl.BlockSpec((TM_PROJ, D_MODEL), lambda i: (i, 0))
    return pl.pallas_call(
        _proj_ln_kernel,
        grid=(t // TM_PROJ,),
        in_specs=[tile, pl.BlockSpec((TM_PROJ, a.shape[1]), lambda i: (i, 0)),
                  _resident(w.shape), _resident((1, D_MODEL)), _resident((1, D_MODEL))],
        out_specs=tile,
        out_shape=jax.ShapeDtypeStruct((t, D_MODEL), F32),
        compiler_params=_params(("parallel",), 32),
        name="proj_ln",
    )(res, a, w, g, b)


def _lru_kernel(x_ref, win_ref, cw_ref, cb_ref, wra_ref, bra_ref, wrx_ref, brx_ref, lam_ref,
                wout_ref, g_ref, b_ref, o_ref, xpad_ref, a_ref, bb_ref, h_ref, carry_ref):
    tm = TM_LRU

    @pl.when(pl.program_id(1) == 0)
    def _():
        xpad_ref[0:SUBLANES, :] = jnp.zeros((SUBLANES, D_RNN), F32)
        carry_ref[...] = jnp.zeros((1, D_RNN), F32)

    x = x_ref[...]
    z = jnp.dot(x.astype(BF16), win_ref[...], preferred_element_type=F32)
    xb = z[:, :D_RNN]
    gate = jax.nn.gelu(z[:, D_RNN:])

    xpad_ref[SUBLANES:, :] = xb
    cw = cw_ref[...]
    xc = cb_ref[...]
    for w in range(CONV_W):
        off = SUBLANES - (CONV_W - 1) + w
        xc = xc + cw[w:w + 1, :] * xpad_ref[off:off + tm, :]
    xpad_ref[0:SUBLANES, :] = xb[tm - SUBLANES:, :]

    lam = lam_ref[...]
    log_sig = jnp.minimum(lam, 0.0) - jnp.log1p(jnp.exp(-jnp.abs(lam)))
    for n in range(RNN_BLOCKS):
        sl = slice(n * RNN_BLOCK_W, (n + 1) * RNN_BLOCK_W)
        xn = xc[:, sl]
        xr = xn.astype(BF16)
        r = jax.nn.sigmoid(jnp.dot(xr, wra_ref[n], preferred_element_type=F32) + bra_ref[:, sl])
        i = jax.nn.sigmoid(jnp.dot(xr, wrx_ref[n], preferred_element_type=F32) + brx_ref[:, sl])
        log_a = LRU_C * r * log_sig[:, sl]
        a_ref[:, sl] = jnp.exp(log_a)
        th = jnp.tanh(log_a)
        bb_ref[:, sl] = jnp.sqrt(-2.0 * th / (1.0 - th)) * (i * xn)

    row = lax.broadcasted_iota(jnp.int32, (SUBLANES, D_RNN), 0)

    def group(gi, h_in):
        r0 = pl.multiple_of(gi * SUBLANES, SUBLANES)
        a = a_ref[pl.ds(r0, SUBLANES), :]
        b = bb_ref[pl.ds(r0, SUBLANES), :]
        for d in (1, 2, 4):
            a_sh = jnp.where(row >= d, pltpu.roll(a, d, 0), 1.0)
            b_sh = jnp.where(row >= d, pltpu.roll(b, d, 0), 0.0)
            b = a * b_sh + b
            a = a * a_sh
        h = a * h_in + b
        h_ref[pl.ds(r0, SUBLANES), :] = h
        return h[SUBLANES - 1:SUBLANES, :]

    carry_ref[...] = lax.fori_loop(0, tm // SUBLANES, group, carry_ref[...])

    y = (h_ref[...] * gate).astype(BF16)
    out = ALPHA * x + jnp.dot(y, wout_ref[...], preferred_element_type=F32)
    o_ref[...] = _layernorm(out, g_ref[...], b_ref[...])


def _lru_ln(x, win, cw, cb, wra, bra, wrx, brx, lam, wout, g, b, batch, seq):
    tiles = seq // TM_LRU
    tile = pl.BlockSpec((TM_LRU, D_MODEL), lambda bi, ti: (bi * tiles + ti, 0))
    vec = _resident((1, D_RNN))
    gates = _resident((RNN_BLOCKS, RNN_BLOCK_W, RNN_BLOCK_W))
    return pl.pallas_call(
        _lru_kernel,
        grid=(batch, tiles),
        in_specs=[tile, _resident((D_MODEL, 2 * D_RNN)), _resident((CONV_W, D_RNN)), vec,
                  gates, vec, gates, vec, vec, _resident((D_RNN, D_MODEL)),
                  _resident((1, D_MODEL)), _resident((1, D_MODEL))],
        out_specs=tile,
        out_shape=jax.ShapeDtypeStruct((batch * seq, D_MODEL), F32),
        scratch_shapes=[pltpu.VMEM((TM_LRU + SUBLANES, D_RNN), F32),
                        pltpu.VMEM((TM_LRU, D_RNN), F32),
                        pltpu.VMEM((TM_LRU, D_RNN), F32),
                        pltpu.VMEM((TM_LRU, D_RNN), F32),
                        pltpu.VMEM((1, D_RNN), F32)],
        compiler_params=_params(("arbitrary", "arbitrary"), 48),
        name="lru_ln",
    )(x, win, cw, cb, wra, bra, wrx, brx, lam, wout, g, b)


def _rope_tables(seq):
    pos = jnp.arange(seq, dtype=F32)
    inv_freq = ROPE_THETA ** (-jnp.arange(0, HEAD_DIM, 2, dtype=F32) / HEAD_DIM)
    ang = pos[:, None] * inv_freq[None, :]
    cos, sin = jnp.cos(ang), jnp.sin(ang)
    zero = jnp.zeros_like(sin)
    cos_t = jnp.concatenate([cos, cos, cos, cos], axis=-1)
    sin_a = jnp.concatenate([-sin, zero, -sin, zero], axis=-1)
    sin_b = jnp.concatenate([zero, sin, zero, sin], axis=-1)
    return cos_t, sin_a, sin_b


def _permute_q_heads(w_qkv, w_o):
    nq = N_HEADS * HEAD_DIM
    perm = jnp.asarray(Q_HEAD_PERM)
    wq = w_qkv[:, :nq].reshape(D_MODEL, N_HEADS, HEAD_DIM)[:, perm, :].reshape(D_MODEL, nq)
    w_qkv_p = jnp.concatenate([wq, w_qkv[:, nq:]], axis=1)
    w_o_p = w_o.reshape(N_HEADS, HEAD_DIM, D_MODEL)[perm].reshape(nq, D_MODEL)
    return w_qkv_p, w_o_p


def kernel(x, ffn1_w_in, ffn1_w_out, ffn2_w_in, ffn2_w_out, ln_g, ln_b, attn_w_qkv, attn_sinks,
           attn_w_o, lru_w_in, lru_conv_w, lru_conv_b, lru_w_ra, lru_b_ra, lru_w_rx, lru_b_rx,
           lru_lambda, lru_w_out):
    batch, seq, _ = x.shape
    cos_t, sin_a, sin_b = _rope_tables(seq)
    h = x.reshape(batch * seq, D_MODEL)
    vec = lambda a: a.reshape(1, -1)
    for i in range(DEPTH):
        h = _ffn_ln(h, ffn1_w_in[i].astype(BF16), ffn1_w_out[i].astype(BF16),
                    vec(ln_g[i, 0]), vec(ln_b[i, 0]))
        j = i // 2
        if i % 2 == 0:
            w_qkv, w_o = _permute_q_heads(attn_w_qkv[j], attn_w_o[j])
            q, k, v = _qkv_rope(h, w_qkv.astype(BF16), cos_t, sin_a, sin_b, seq)
            o = _swa_attn(attn_sinks[j], q, k, v, seq)
            h = _proj_ln(h, o, w_o.astype(BF16), vec(ln_g[i, 1]), vec(ln_b[i, 1]))
        else:
            h = _lru_ln(h, lru_w_in[j].astype(BF16), lru_conv_w[j], vec(lru_conv_b[j]),
                        lru_w_ra[j].astype(BF16), vec(lru_b_ra[j]),
                        lru_w_rx[j].astype(BF16), vec(lru_b_rx[j]), vec(lru_lambda[j]),
                        lru_w_out[j].astype(BF16), vec(ln_g[i, 1]), vec(ln_b[i, 1]), batch, seq)
        h = _ffn_ln(h, ffn2_w_in[i].astype(BF16), ffn2_w_out[i].astype(BF16),
                    vec(ln_g[i, 2]), vec(ln_b[i, 2]))
    return h.reshape(batch, seq, D_MODEL)
```

```python
import functools

import jax
import jax.numpy as jnp
from jax import lax
from jax.experimental import pallas as pl
from jax.experimental.pallas import tpu as pltpu

F32 = jnp.float32
BF16 = jnp.bfloat16

D_MODEL = 1024
DEPTH = 4
N_HEADS = 16
N_KV_HEADS = 4
HEAD_DIM = 64
GROUP = N_HEADS // N_KV_HEADS
WINDOW = 128
ROPE_THETA = 10000.0
D_RNN = 1024
RNN_BLOCKS = 4
RNN_BLOCK_W = D_RNN // RNN_BLOCKS
CONV_W = 4
LRU_C = 8.0
D_FF = 2816
ALPHA = (2.0 * DEPTH) ** 0.25
LN_EPS = 1e-5
QKV_COLS = (N_HEADS + 2 * N_KV_HEADS) * HEAD_DIM

LANES = 128
SUBLANES = 8
MIB = 1024 * 1024

TM_FFN = 512
FF_CHUNK = 256
TM_PROJ = 512
TQ_ATTN = 512
TM_LRU = 512

Q_HEAD_PERM = tuple((2 * kp + half) * GROUP + c
                    for kp in range(N_KV_HEADS // 2) for c in range(GROUP) for half in range(2))


def _resident(shape, layer):
    nd = len(shape)
    return pl.BlockSpec((None,) + tuple(shape), lambda *_: (layer,) + (0,) * nd,
                        pipeline_mode=pl.Buffered(1))


def _sigmoid(x):
    return 0.5 * jnp.tanh(0.5 * x) + 0.5


def _params(semantics, vmem_mib):
    return pltpu.CompilerParams(dimension_semantics=semantics, vmem_limit_bytes=vmem_mib * MIB)


def _layernorm(y, g, b):
    mu = jnp.mean(y, axis=-1, keepdims=True)
    yc = y - mu
    var = jnp.mean(yc * yc, axis=-1, keepdims=True)
    return yc * lax.rsqrt(var + LN_EPS) * g + b


def _ffn_kernel(x_ref, wi_ref, wo_ref, g_ref, b_ref, o_ref, acc_ref):
    x = x_ref[...]
    xb = x.astype(BF16)
    for c in range(D_FF // FF_CHUNK):
        lo = c * FF_CHUNK
        gate = jnp.dot(xb, wi_ref[:, lo:lo + FF_CHUNK], preferred_element_type=F32)
        up = jnp.dot(xb, wi_ref[:, D_FF + lo:D_FF + lo + FF_CHUNK], preferred_element_type=F32)
        act = (jax.nn.silu(gate) * up).astype(BF16)
        part = jnp.dot(act, wo_ref[lo:lo + FF_CHUNK, :], preferred_element_type=F32)
        if c == 0:
            acc_ref[...] = part
        else:
            acc_ref[...] += part
    y = ALPHA * x + 0.5 * acc_ref[...]
    o_ref[...] = _layernorm(y, g_ref[...], b_ref[...])


def _ffn_ln(x, wi, wo, ln_g, ln_b, layer, ln_idx):
    t = x.shape[0]
    tile = pl.BlockSpec((TM_FFN, D_MODEL), lambda i: (i, 0))
    return pl.pallas_call(
        _ffn_kernel,
        grid=(t // TM_FFN,),
        in_specs=[tile, _resident((D_MODEL, 2 * D_FF), layer), _resident((D_FF, D_MODEL), layer),
                  _resident((1, D_MODEL), ln_idx), _resident((1, D_MODEL), ln_idx)],
        out_specs=tile,
        out_shape=jax.ShapeDtypeStruct((t, D_MODEL), F32),
        scratch_shapes=[pltpu.VMEM((TM_FFN, D_MODEL), F32)],
        compiler_params=_params(("parallel",), 48),
        name="ffn_ln",
    )(x, wi, wo, ln_g, ln_b)


def _qkv_kernel(x_ref, w_ref, cos_ref, sa_ref, sb_ref, q_ref, k_ref, v_ref):
    z = jnp.dot(x_ref[...].astype(BF16), w_ref[...], preferred_element_type=F32)
    cos, sa, sb = cos_ref[...], sa_ref[...], sb_ref[...]

    def rope(t):
        return t * cos + pltpu.roll(t, LANES - 32, 1) * sa + pltpu.roll(t, 32, 1) * sb

    nq = N_HEADS * HEAD_DIM
    nk = N_KV_HEADS * HEAD_DIM
    for c in range(nq // LANES):
        sl = slice(c * LANES, (c + 1) * LANES)
        q_ref[:, sl] = (rope(z[:, sl]) * (HEAD_DIM ** -0.5)).astype(BF16)
    for c in range(nk // LANES):
        sl = slice(c * LANES, (c + 1) * LANES)
        k_ref[:, sl] = rope(z[:, nq + c * LANES:nq + (c + 1) * LANES]).astype(BF16)
    v_ref[...] = z[:, nq + nk:].astype(BF16)


def _qkv_rope(x, w, cos, sa, sb, seq, layer):
    t = x.shape[0]
    nq = N_HEADS * HEAD_DIM
    nk = N_KV_HEADS * HEAD_DIM
    tiles_per_seq = seq // TM_PROJ
    tab = pl.BlockSpec((TM_PROJ, LANES), lambda i: (i % tiles_per_seq, 0))
    row = lambda n: pl.BlockSpec((TM_PROJ, n), lambda i: (i, 0))
    return pl.pallas_call(
        _qkv_kernel,
        grid=(t // TM_PROJ,),
        in_specs=[row(D_MODEL), _resident((D_MODEL, QKV_COLS), layer), tab, tab, tab],
        out_specs=[row(nq), row(nk), row(nk)],
        out_shape=[jax.ShapeDtypeStruct((t, nq), BF16), jax.ShapeDtypeStruct((t, nk), BF16),
                   jax.ShapeDtypeStruct((t, nk), BF16)],
        compiler_params=_params(("parallel",), 32),
        name="qkv_rope",
    )(x, w, cos, sa, sb)


def _attn_kernel(sink_ref, q_ref, kc_ref, kp_ref, vc_ref, vp_ref, o_ref, *, tiles_per_seq):
    nblk = TQ_ATTN // WINDOW
    first = (pl.program_id(0) % tiles_per_seq) == 0
    kfull = jnp.concatenate([kp_ref[...], kc_ref[...]], axis=0)
    vfull = jnp.concatenate([vp_ref[...], vc_ref[...]], axis=0)

    qi = lax.broadcasted_iota(jnp.int32, (WINDOW, 2 * WINDOW), 0)
    kj = lax.broadcasted_iota(jnp.int32, (WINDOW, 2 * WINDOW), 1)
    band = (kj > qi) & (kj <= qi + WINDOW)
    band0 = band & (kj >= jnp.where(first, WINDOW, 0))
    lane_k = lax.broadcasted_iota(jnp.int32, (2 * WINDOW, LANES), 1)
    lane_o = lax.broadcasted_iota(jnp.int32, (WINDOW, LANES), 1)
    ones = jnp.ones((2 * WINDOW, LANES), BF16)
    neg = jnp.finfo(F32).min

    for j in range(nblk):
        mask = band0 if j == 0 else band
        rows = slice(j * WINDOW, (j + 1) * WINDOW)
        keys = slice(j * WINDOW, (j + 2) * WINDOW)
        for kp in range(N_KV_HEADS // 2):
            pair = slice(kp * LANES, (kp + 1) * LANES)
            kblk = kfull[keys, pair]
            k_halves = (jnp.where(lane_k < HEAD_DIM, kblk, jnp.zeros_like(kblk)),
                        jnp.where(lane_k >= HEAD_DIM, kblk, jnp.zeros_like(kblk)))
            qstack = jnp.concatenate(
                [q_ref[rows, (kp * GROUP + c) * LANES:(kp * GROUP + c + 1) * LANES]
                 for c in range(GROUP)], axis=0)
            probs, sink_terms = [], []
            for half in range(2):
                s = lax.dot_general(qstack, k_halves[half], (((1,), (1,)), ((), ())),
                                    preferred_element_type=F32)
                for c in range(GROUP):
                    sink = sink_ref[(2 * kp + half) * GROUP + c]
                    sc = jnp.where(mask, s[c * WINDOW:(c + 1) * WINDOW, :], neg)
                    m = jnp.maximum(jnp.max(sc, axis=-1, keepdims=True), sink)
                    probs.append(jnp.exp(sc - m).astype(BF16))
                    sink_terms.append(jnp.exp(sink - m))
            p = jnp.concatenate(probs, axis=0)
            vext = jnp.concatenate([vfull[keys, pair], ones], axis=1)
            pv = jnp.dot(p, vext, preferred_element_type=F32)
            for c in range(GROUP):
                nums, dens = [], []
                for half in range(2):
                    r = (half * GROUP + c) * WINDOW
                    nums.append(pv[r:r + WINDOW, :LANES])
                    dens.append(pv[r:r + WINDOW, LANES:] + sink_terms[half * GROUP + c])
                low = lane_o < HEAD_DIM
                chunk = jnp.where(low, nums[0], nums[1]) / jnp.where(low, dens[0], dens[1])
                col = (kp * GROUP + c) * LANES
                o_ref[rows, col:col + LANES] = chunk.astype(BF16)


def _swa_attn(sinks, q, k, v, seq):
    t = q.shape[0]
    nq = N_HEADS * HEAD_DIM
    nk = N_KV_HEADS * HEAD_DIM
    blk_per_tile = TQ_ATTN // WINDOW
    cur = lambda n: pl.BlockSpec((TQ_ATTN, n), lambda i: (i, 0))
    prev = pl.BlockSpec((WINDOW, nk), lambda i: (jnp.maximum(i * blk_per_tile - 1, 0), 0))
    return pl.pallas_call(
        functools.partial(_attn_kernel, tiles_per_seq=seq // TQ_ATTN),
        grid=(t // TQ_ATTN,),
        in_specs=[pl.BlockSpec(memory_space=pltpu.SMEM), cur(nq), cur(nk), prev, cur(nk), prev],
        out_specs=cur(nq),
        out_shape=jax.ShapeDtypeStruct((t, nq), BF16),
        compiler_params=_params(("parallel",), 32),
        name="swa_attn",
    )(sinks, q, k, k, v, v)


def _proj_ln_kernel(res_ref, a_ref, w_ref, g_ref, b_ref, o_ref):
    y = ALPHA * res_ref[...] + jnp.dot(a_ref[...], w_ref[...], preferred_element_type=F32)
    o_ref[...] = _layernorm(y, g_ref[...], b_ref[...])


def _proj_ln(res, a, w, ln_g, ln_b, layer, ln_idx):
    t = res.shape[0]
    tile = pl.BlockSpec((TM_PROJ, D_MODEL), lambda i: (i, 0))
    return pl.pallas_call(
        _proj_ln_kernel,
        grid=(t // TM_PROJ,),
        in_specs=[tile, pl.BlockSpec((TM_PROJ, a.shape[1]), lambda i: (i, 0)),
                  _resident(w.shape[1:], layer),
                  _resident((1, D_MODEL), ln_idx), _resident((1, D_MODEL), ln_idx)],
        out_specs=tile,
        out_shape=jax.ShapeDtypeStruct((t, D_MODEL), F32),
        compiler_params=_params(("parallel",), 32),
        name="proj_ln",
    )(res, a, w, ln_g, ln_b)


def _lru_kernel(x_ref, win_ref, cw_ref, cb_ref, wra_ref, bra_ref, wrx_ref, brx_ref, lam_ref,
                wout_ref, g_ref, b_ref, o_ref, xpad_ref, a_ref, bb_ref, h_ref, carry_ref):
    tm = TM_LRU

    @pl.when(pl.program_id(1) == 0)
    def _():
        xpad_ref[0:SUBLANES, :] = jnp.zeros((SUBLANES, D_RNN), F32)
        carry_ref[...] = jnp.zeros((1, D_RNN), F32)

    x = x_ref[...]
    z = jnp.dot(x.astype(BF16), win_ref[...], preferred_element_type=F32)
    xb = z[:, :D_RNN]
    gate = jax.nn.gelu(z[:, D_RNN:])

    xpad_ref[SUBLANES:, :] = xb
    cw = cw_ref[...]
    xc = cb_ref[...]
    for w in range(CONV_W):
        off = SUBLANES - (CONV_W - 1) + w
        xc = xc + cw[w:w + 1, :] * xpad_ref[off:off + tm, :]
    xpad_ref[0:SUBLANES, :] = xb[tm - SUBLANES:, :]

    lam = lam_ref[...]
    log_sig = jnp.minimum(lam, 0.0) - jnp.log1p(jnp.exp(-jnp.abs(lam)))
    for n in range(RNN_BLOCKS):
        sl = slice(n * RNN_BLOCK_W, (n + 1) * RNN_BLOCK_W)
        xn = xc[:, sl]
        xr = xn.astype(BF16)
        r = _sigmoid(jnp.dot(xr, wra_ref[n], preferred_element_type=F32) + bra_ref[:, sl])
        i = _sigmoid(jnp.dot(xr, wrx_ref[n], preferred_element_type=F32) + brx_ref[:, sl])
        log_a = LRU_C * r * log_sig[:, sl]
        a = jnp.exp(log_a)
        a_ref[:, sl] = a
        bb_ref[:, sl] = jnp.sqrt(-jnp.tanh(log_a) * (a * a + 1.0)) * (i * xn)

    row = lax.broadcasted_iota(jnp.int32, (SUBLANES, D_RNN), 0)

    def group(gi, h_in):
        r0 = pl.multiple_of(gi * SUBLANES, SUBLANES)
        a = a_ref[pl.ds(r0, SUBLANES), :]
        b = bb_ref[pl.ds(r0, SUBLANES), :]
        for d in (1, 2, 4):
            a_sh = jnp.where(row >= d, pltpu.roll(a, d, 0), 1.0)
            b_sh = jnp.where(row >= d, pltpu.roll(b, d, 0), 0.0)
            b = a * b_sh + b
            a = a * a_sh
        h = a * h_in + b
        h_ref[pl.ds(r0, SUBLANES), :] = h
        return h[SUBLANES - 1:SUBLANES, :]

    carry_ref[...] = lax.fori_loop(0, tm // SUBLANES, group, carry_ref[...])

    y = (h_ref[...] * gate).astype(BF16)
    out = ALPHA * x + jnp.dot(y, wout_ref[...], preferred_element_type=F32)
    o_ref[...] = _layernorm(out, g_ref[...], b_ref[...])


def _lru_ln(x, win, cw, cb, wra, bra, wrx, brx, lam, wout, ln_g, ln_b, batch, seq, layer, ln_idx):
    tiles = seq // TM_LRU
    tile = pl.BlockSpec((TM_LRU, D_MODEL), lambda bi, ti: (bi * tiles + ti, 0))
    vec = _resident((1, D_RNN), layer)
    gates = _resident((RNN_BLOCKS, RNN_BLOCK_W, RNN_BLOCK_W), layer)
    return pl.pallas_call(
        _lru_kernel,
        grid=(batch, tiles),
        in_specs=[tile, _resident((D_MODEL, 2 * D_RNN), layer), _resident((CONV_W, D_RNN), layer),
                  vec, gates, vec, gates, vec, vec, _resident((D_RNN, D_MODEL), layer),
                  _resident((1, D_MODEL), ln_idx), _resident((1, D_MODEL), ln_idx)],
        out_specs=tile,
        out_shape=jax.ShapeDtypeStruct((batch * seq, D_MODEL), F32),
        scratch_shapes=[pltpu.VMEM((TM_LRU + SUBLANES, D_RNN), F32),
                        pltpu.VMEM((TM_LRU, D_RNN), F32),
                        pltpu.VMEM((TM_LRU, D_RNN), F32),
                        pltpu.VMEM((TM_LRU, D_RNN), F32),
                        pltpu.VMEM((1, D_RNN), F32)],
        compiler_params=_params(("arbitrary", "arbitrary"), 48),
        name="lru_ln",
    )(x, win, cw, cb, wra, bra, wrx, brx, lam, wout, ln_g, ln_b)


def _rope_tables(seq):
    pos = jnp.arange(seq, dtype=F32)
    inv_freq = ROPE_THETA ** (-jnp.arange(0, HEAD_DIM, 2, dtype=F32) / HEAD_DIM)
    ang = pos[:, None] * inv_freq[None, :]
    cos, sin = jnp.cos(ang), jnp.sin(ang)
    zero = jnp.zeros_like(sin)
    cos_t = jnp.concatenate([cos, cos, cos, cos], axis=-1)
    sin_a = jnp.concatenate([-sin, zero, -sin, zero], axis=-1)
    sin_b = jnp.concatenate([zero, sin, zero, sin], axis=-1)
    return cos_t, sin_a, sin_b


def _permute_q_heads(w_qkv, w_o):
    nq = N_HEADS * HEAD_DIM
    n = w_qkv.shape[0]
    perm = jnp.asarray(Q_HEAD_PERM)
    wq = w_qkv[:, :, :nq].reshape(n, D_MODEL, N_HEADS, HEAD_DIM)[:, :, perm, :]
    w_qkv_p = jnp.concatenate([wq.reshape(n, D_MODEL, nq), w_qkv[:, :, nq:]], axis=2)
    w_o_p = w_o.reshape(n, N_HEADS, HEAD_DIM, D_MODEL)[:, perm].reshape(n, nq, D_MODEL)
    return w_qkv_p, w_o_p


def kernel(x, ffn1_w_in, ffn1_w_out, ffn2_w_in, ffn2_w_out, ln_g, ln_b, attn_w_qkv, attn_sinks,
           attn_w_o, lru_w_in, lru_conv_w, lru_conv_b, lru_w_ra, lru_b_ra, lru_w_rx, lru_b_rx,
           lru_lambda, lru_w_out):
    batch, seq, _ = x.shape
    cos_t, sin_a, sin_b = _rope_tables(seq)
    bf = lambda a: a.astype(BF16)
    vecs = lambda a: a.reshape(-1, 1, a.shape[-1])
    ffn_w = ((bf(ffn1_w_in), bf(ffn1_w_out)), (bf(ffn2_w_in), bf(ffn2_w_out)))
    w_qkv, w_o = (bf(w) for w in _permute_q_heads(attn_w_qkv, attn_w_o))
    lru = (bf(lru_w_in), lru_conv_w, vecs(lru_conv_b), bf(lru_w_ra), vecs(lru_b_ra),
           bf(lru_w_rx), vecs(lru_b_rx), vecs(lru_lambda), bf(lru_w_out))
    g, b = vecs(ln_g), vecs(ln_b)

    h = x.reshape(batch * seq, D_MODEL)
    for i in range(DEPTH):
        h = _ffn_ln(h, *ffn_w[0], g, b, i, 3 * i)
        j = i // 2
        if i % 2 == 0:
            q, k, v = _qkv_rope(h, w_qkv, cos_t, sin_a, sin_b, seq, j)
            o = _swa_attn(attn_sinks[j], q, k, v, seq)
            h = _proj_ln(h, o, w_o, g, b, j, 3 * i + 1)
        else:
            h = _lru_ln(h, *lru, g, b, batch, seq, j, 3 * i + 1)
        h = _ffn_ln(h, *ffn_w[1], g, b, i, 3 * i + 2)
    return h.reshape(batch, seq, D_MODEL)
```

```python
import functools

import jax
import jax.numpy as jnp
from jax import lax
from jax.experimental import pallas as pl
from jax.experimental.pallas import tpu as pltpu

F32 = jnp.float32
BF16 = jnp.bfloat16

D_MODEL = 1024
DEPTH = 4
N_HEADS = 16
N_KV_HEADS = 4
HEAD_DIM = 64
GROUP = N_HEADS // N_KV_HEADS
WINDOW = 128
ROPE_THETA = 10000.0
D_RNN = 1024
RNN_BLOCKS = 4
RNN_BLOCK_W = D_RNN // RNN_BLOCKS
CONV_W = 4
LRU_C = 8.0
D_FF = 2816
ALPHA = (2.0 * DEPTH) ** 0.25
LN_EPS = 1e-5
QKV_COLS = (N_HEADS + 2 * N_KV_HEADS) * HEAD_DIM

LANES = 128
SUBLANES = 8
MIB = 1024 * 1024

TM_FFN = 1024
SUB_FFN = 512
FF_CHUNKS = (256,) * 11
assert sum(FF_CHUNKS) == D_FF
TM_PROJ = 1024
TQ_ATTN = 512
TM_LRU = 512

Q_HEAD_PERM = tuple((2 * kp + half) * GROUP + c
                    for kp in range(N_KV_HEADS // 2) for c in range(GROUP) for half in range(2))


def _resident(shape, layer):
    nd = len(shape)
    return pl.BlockSpec((None,) + tuple(shape), lambda *_: (layer,) + (0,) * nd,
                        pipeline_mode=pl.Buffered(1))


def _sigmoid(x):
    return 0.5 * jnp.tanh(0.5 * x) + 0.5


def _params(semantics, vmem_mib):
    return pltpu.CompilerParams(dimension_semantics=semantics, vmem_limit_bytes=vmem_mib * MIB)


def _layernorm(y, g, b):
    mu = jnp.mean(y, axis=-1, keepdims=True)
    yc = y - mu
    var = jnp.mean(yc * yc, axis=-1, keepdims=True)
    return yc * lax.rsqrt(var + LN_EPS) * g + b


def _ffn_kernel(x_ref, wi_ref, wo_ref, g_ref, b_ref, o_ref, acc_ref):
    for s in range(TM_FFN // SUB_FFN):
        rows = slice(s * SUB_FFN, (s + 1) * SUB_FFN)
        x = x_ref[rows, :]
        xb = x.astype(BF16)
        lo = 0
        for width in FF_CHUNKS:
            gate = jnp.dot(xb, wi_ref[:, lo:lo + width], preferred_element_type=F32)
            up = jnp.dot(xb, wi_ref[:, D_FF + lo:D_FF + lo + width], preferred_element_type=F32)
            act = (jax.nn.silu(gate) * up).astype(BF16)
            part = jnp.dot(act, wo_ref[lo:lo + width, :], preferred_element_type=F32)
            if lo == 0:
                acc_ref[rows, :] = part
            else:
                acc_ref[rows, :] += part
            lo += width
        y = ALPHA * x + 0.5 * acc_ref[rows, :]
        o_ref[rows, :] = _layernorm(y, g_ref[...], b_ref[...])


def _ffn_ln(x, wi, wo, ln_g, ln_b, layer, ln_idx):
    t = x.shape[0]
    tile = pl.BlockSpec((TM_FFN, D_MODEL), lambda i: (i, 0))
    return pl.pallas_call(
        _ffn_kernel,
        grid=(t // TM_FFN,),
        in_specs=[tile, _resident((D_MODEL, 2 * D_FF), layer), _resident((D_FF, D_MODEL), layer),
                  _resident((1, D_MODEL), ln_idx), _resident((1, D_MODEL), ln_idx)],
        out_specs=tile,
        out_shape=jax.ShapeDtypeStruct((t, D_MODEL), F32),
        scratch_shapes=[pltpu.VMEM((TM_FFN, D_MODEL), F32)],
        compiler_params=_params(("parallel",), 56),
        name="ffn_ln",
    )(x, wi, wo, ln_g, ln_b)


def _qkv_kernel(x_ref, w_ref, cos_ref, sa_ref, sb_ref, q_ref, k_ref, v_ref):
    z = jnp.dot(x_ref[...].astype(BF16), w_ref[...], preferred_element_type=F32)
    cos, sa, sb = cos_ref[...], sa_ref[...], sb_ref[...]

    def rope(t):
        return t * cos + pltpu.roll(t, LANES - 32, 1) * sa + pltpu.roll(t, 32, 1) * sb

    nq = N_HEADS * HEAD_DIM
    nk = N_KV_HEADS * HEAD_DIM
    for c in range(nq // LANES):
        sl = slice(c * LANES, (c + 1) * LANES)
        q_ref[:, sl] = (rope(z[:, sl]) * (HEAD_DIM ** -0.5)).astype(BF16)
    for c in range(nk // LANES):
        sl = slice(c * LANES, (c + 1) * LANES)
        k_ref[:, sl] = rope(z[:, nq + c * LANES:nq + (c + 1) * LANES]).astype(BF16)
    v_ref[...] = z[:, nq + nk:].astype(BF16)


def _qkv_rope(x, w, cos, sa, sb, seq, layer):
    t = x.shape[0]
    nq = N_HEADS * HEAD_DIM
    nk = N_KV_HEADS * HEAD_DIM
    tiles_per_seq = seq // TM_PROJ
    tab = pl.BlockSpec((TM_PROJ, LANES), lambda i: (i % tiles_per_seq, 0))
    row = lambda n: pl.BlockSpec((TM_PROJ, n), lambda i: (i, 0))
    return pl.pallas_call(
        _qkv_kernel,
        grid=(t // TM_PROJ,),
        in_specs=[row(D_MODEL), _resident((D_MODEL, QKV_COLS), layer), tab, tab, tab],
        out_specs=[row(nq), row(nk), row(nk)],
        out_shape=[jax.ShapeDtypeStruct((t, nq), BF16), jax.ShapeDtypeStruct((t, nk), BF16),
                   jax.ShapeDtypeStruct((t, nk), BF16)],
        compiler_params=_params(("parallel",), 32),
        name="qkv_rope",
    )(x, w, cos, sa, sb)


def _attn_kernel(sink_ref, q_ref, kc_ref, kp_ref, vc_ref, vp_ref, o_ref, *, tiles_per_seq):
    nblk = TQ_ATTN // WINDOW
    first = (pl.program_id(0) % tiles_per_seq) == 0
    kfull = jnp.concatenate([kp_ref[...], kc_ref[...]], axis=0)
    vfull = jnp.concatenate([vp_ref[...], vc_ref[...]], axis=0)

    qi = lax.broadcasted_iota(jnp.int32, (WINDOW, 2 * WINDOW), 0)
    kj = lax.broadcasted_iota(jnp.int32, (WINDOW, 2 * WINDOW), 1)
    band = (kj > qi) & (kj <= qi + WINDOW)
    band0 = band & (kj >= jnp.where(first, WINDOW, 0))
    lane_k = lax.broadcasted_iota(jnp.int32, (2 * WINDOW, LANES), 1)
    lane_o = lax.broadcasted_iota(jnp.int32, (WINDOW, LANES), 1)
    ones = jnp.ones((2 * WINDOW, LANES), BF16)
    neg = jnp.finfo(F32).min

    for j in range(nblk):
        mask = band0 if j == 0 else band
        rows = slice(j * WINDOW, (j + 1) * WINDOW)
        keys = slice(j * WINDOW, (j + 2) * WINDOW)
        for kp in range(N_KV_HEADS // 2):
            pair = slice(kp * LANES, (kp + 1) * LANES)
            kblk = kfull[keys, pair]
            k_halves = (jnp.where(lane_k < HEAD_DIM, kblk, jnp.zeros_like(kblk)),
                        jnp.where(lane_k >= HEAD_DIM, kblk, jnp.zeros_like(kblk)))
            qstack = jnp.concatenate(
                [q_ref[rows, (kp * GROUP + c) * LANES:(kp * GROUP + c + 1) * LANES]
                 for c in range(GROUP)], axis=0)
            probs, sink_terms = [], []
            for half in range(2):
                s = lax.dot_general(qstack, k_halves[half], (((1,), (1,)), ((), ())),
                                    preferred_element_type=F32)
                for c in range(GROUP):
                    sink = sink_ref[(2 * kp + half) * GROUP + c]
                    sc = jnp.where(mask, s[c * WINDOW:(c + 1) * WINDOW, :], neg)
                    m = jnp.maximum(jnp.max(sc, axis=-1, keepdims=True), sink)
                    probs.append(jnp.exp(sc - m).astype(BF16))
                    sink_terms.append(jnp.exp(sink - m))
            p = jnp.concatenate(probs, axis=0)
            vext = jnp.concatenate([vfull[keys, pair], ones], axis=1)
            pv = jnp.dot(p, vext, preferred_element_type=F32)
            for c in range(GROUP):
                nums, dens = [], []
                for half in range(2):
                    r = (half * GROUP + c) * WINDOW
                    nums.append(pv[r:r + WINDOW, :LANES])
                    dens.append(pv[r:r + WINDOW, LANES:] + sink_terms[half * GROUP + c])
                low = lane_o < HEAD_DIM
                chunk = jnp.where(low, nums[0], nums[1]) / jnp.where(low, dens[0], dens[1])
                col = (kp * GROUP + c) * LANES
                o_ref[rows, col:col + LANES] = chunk.astype(BF16)


def _swa_attn(sinks, q, k, v, seq):
    t = q.shape[0]
    nq = N_HEADS * HEAD_DIM
    nk = N_KV_HEADS * HEAD_DIM
    blk_per_tile = TQ_ATTN // WINDOW
    cur = lambda n: pl.BlockSpec((TQ_ATTN, n), lambda i: (i, 0))
    prev = pl.BlockSpec((WINDOW, nk), lambda i: (jnp.maximum(i * blk_per_tile - 1, 0), 0))
    return pl.pallas_call(
        functools.partial(_attn_kernel, tiles_per_seq=seq // TQ_ATTN),
        grid=(t // TQ_ATTN,),
        in_specs=[pl.BlockSpec(memory_space=pltpu.SMEM), cur(nq), cur(nk), prev, cur(nk), prev],
        out_specs=cur(nq),
        out_shape=jax.ShapeDtypeStruct((t, nq), BF16),
        compiler_params=_params(("parallel",), 32),
        name="swa_attn",
    )(sinks, q, k, k, v, v)


def _proj_ln_kernel(res_ref, a_ref, w_ref, g_ref, b_ref, o_ref):
    y = ALPHA * res_ref[...] + jnp.dot(a_ref[...], w_ref[...], preferred_element_type=F32)
    o_ref[...] = _layernorm(y, g_ref[...], b_ref[...])


def _proj_ln(res, a, w, ln_g, ln_b, layer, ln_idx):
    t = res.shape[0]
    tile = pl.BlockSpec((TM_PROJ, D_MODEL), lambda i: (i, 0))
    return pl.pallas_call(
        _proj_ln_kernel,
        grid=(t // TM_PROJ,),
        in_specs=[tile, pl.BlockSpec((TM_PROJ, a.shape[1]), lambda i: (i, 0)),
                  _resident(w.shape[1:], layer),
                  _resident((1, D_MODEL), ln_idx), _resident((1, D_MODEL), ln_idx)],
        out_specs=tile,
        out_shape=jax.ShapeDtypeStruct((t, D_MODEL), F32),
        compiler_params=_params(("parallel",), 32),
        name="proj_ln",
    )(res, a, w, ln_g, ln_b)


def _lru_kernel(x_ref, win_ref, cw_ref, cb_ref, wra_ref, bra_ref, wrx_ref, brx_ref, lam_ref,
                wout_ref, g_ref, b_ref, o_ref, xpad_ref, a_ref, bb_ref, h_ref, carry_ref):
    tm = TM_LRU

    @pl.when(pl.program_id(1) == 0)
    def _():
        xpad_ref[0:SUBLANES, :] = jnp.zeros((SUBLANES, D_RNN), F32)
        carry_ref[...] = jnp.zeros((1, D_RNN), F32)

    x = x_ref[...]
    z = jnp.dot(x.astype(BF16), win_ref[...], preferred_element_type=F32)
    xb = z[:, :D_RNN]
    gate = jax.nn.gelu(z[:, D_RNN:])

    xpad_ref[SUBLANES:, :] = xb
    cw = cw_ref[...]
    xc = cb_ref[...]
    for w in range(CONV_W):
        off = SUBLANES - (CONV_W - 1) + w
        xc = xc + cw[w:w + 1, :] * xpad_ref[off:off + tm, :]
    xpad_ref[0:SUBLANES, :] = xb[tm - SUBLANES:, :]

    lam = lam_ref[...]
    c_log_sig = LRU_C * (jnp.minimum(lam, 0.0) - jnp.log1p(jnp.exp(-jnp.abs(lam))))
    for n in range(RNN_BLOCKS):
        sl = slice(n * RNN_BLOCK_W, (n + 1) * RNN_BLOCK_W)
        xn = xc[:, sl]
        xr = xn.astype(BF16)
        r = _sigmoid(jnp.dot(xr, wra_ref[n], preferred_element_type=F32) + bra_ref[:, sl])
        i = _sigmoid(jnp.dot(xr, wrx_ref[n], preferred_element_type=F32) + brx_ref[:, sl])
        log_a = r * c_log_sig[:, sl]
        a = jnp.exp(log_a)
        a_ref[:, sl] = a
        q = -jnp.tanh(log_a) * (a * a + 1.0)
        root = jnp.where(q == 0.0, 0.0, q * lax.rsqrt(q))
        bb_ref[:, sl] = root * (i * xn)

    row = lax.broadcasted_iota(jnp.int32, (SUBLANES, D_RNN), 0)

    def group(gi, h_in):
        r0 = pl.multiple_of(gi * SUBLANES, SUBLANES)
        a = a_ref[pl.ds(r0, SUBLANES), :]
        b = bb_ref[pl.ds(r0, SUBLANES), :]
        for d in (1, 2, 4):
            a_sh = jnp.where(row >= d, pltpu.roll(a, d, 0), 1.0)
            b_sh = jnp.where(row >= d, pltpu.roll(b, d, 0), 0.0)
            b = a * b_sh + b
            a = a * a_sh
        h = a * h_in + b
        h_ref[pl.ds(r0, SUBLANES), :] = h
        return h[SUBLANES - 1:SUBLANES, :]

    carry_ref[...] = lax.fori_loop(0, tm // SUBLANES, group, carry_ref[...])

    y = (h_ref[...] * gate).astype(BF16)
    out = ALPHA * x + jnp.dot(y, wout_ref[...], preferred_element_type=F32)
    o_ref[...] = _layernorm(out, g_ref[...], b_ref[...])


def _lru_ln(x, win, cw, cb, wra, bra, wrx, brx, lam, wout, ln_g, ln_b, batch, seq, layer, ln_idx):
    tiles = seq // TM_LRU
    tile = pl.BlockSpec((TM_LRU, D_MODEL), lambda bi, ti: (bi * tiles + ti, 0))
    vec = _resident((1, D_RNN), layer)
    gates = _resident((RNN_BLOCKS, RNN_BLOCK_W, RNN_BLOCK_W), layer)
    return pl.pallas_call(
        _lru_kernel,
        grid=(batch, tiles),
        in_specs=[tile, _resident((D_MODEL, 2 * D_RNN), layer), _resident((CONV_W, D_RNN), layer),
                  vec, gates, vec, gates, vec, vec, _resident((D_RNN, D_MODEL), layer),
                  _resident((1, D_MODEL), ln_idx), _resident((1, D_MODEL), ln_idx)],
        out_specs=tile,
        out_shape=jax.ShapeDtypeStruct((batch * seq, D_MODEL), F32),
        scratch_shapes=[pltpu.VMEM((TM_LRU + SUBLANES, D_RNN), F32),
                        pltpu.VMEM((TM_LRU, D_RNN), F32),
                        pltpu.VMEM((TM_LRU, D_RNN), F32),
                        pltpu.VMEM((TM_LRU, D_RNN), F32),
                        pltpu.VMEM((1, D_RNN), F32)],
        compiler_params=_params(("arbitrary", "arbitrary"), 48),
        name="lru_ln",
    )(x, win, cw, cb, wra, bra, wrx, brx, lam, wout, ln_g, ln_b)


def _rope_tables(seq):
    pos = jnp.arange(seq, dtype=F32)
    inv_freq = ROPE_THETA ** (-jnp.arange(0, HEAD_DIM, 2, dtype=F32) / HEAD_DIM)
    ang = pos[:, None] * inv_freq[None, :]
    cos, sin = jnp.cos(ang), jnp.sin(ang)
    zero = jnp.zeros_like(sin)
    cos_t = jnp.concatenate([cos, cos, cos, cos], axis=-1)
    sin_a = jnp.concatenate([-sin, zero, -sin, zero], axis=-1)
    sin_b = jnp.concatenate([zero, sin, zero, sin], axis=-1)
    return cos_t, sin_a, sin_b


def _permute_q_heads(w_qkv, w_o):
    nq = N_HEADS * HEAD_DIM
    n = w_qkv.shape[0]
    perm = jnp.asarray(Q_HEAD_PERM)
    wq = w_qkv[:, :, :nq].reshape(n, D_MODEL, N_HEADS, HEAD_DIM)[:, :, perm, :]
    w_qkv_p = jnp.concatenate([wq.reshape(n, D_MODEL, nq), w_qkv[:, :, nq:]], axis=2)
    w_o_p = w_o.reshape(n, N_HEADS, HEAD_DIM, D_MODEL)[:, perm].reshape(n, nq, D_MODEL)
    return w_qkv_p, w_o_p


def kernel(x, ffn1_w_in, ffn1_w_out, ffn2_w_in, ffn2_w_out, ln_g, ln_b, attn_w_qkv, attn_sinks,
           attn_w_o, lru_w_in, lru_conv_w, lru_conv_b, lru_w_ra, lru_b_ra, lru_w_rx, lru_b_rx,
           lru_lambda, lru_w_out):
    batch, seq, _ = x.shape
    cos_t, sin_a, sin_b = _rope_tables(seq)
    bf = lambda a: a.astype(BF16)
    vecs = lambda a: a.reshape(-1, 1, a.shape[-1])
    ffn_w = ((bf(ffn1_w_in), bf(ffn1_w_out)), (bf(ffn2_w_in), bf(ffn2_w_out)))
    w_qkv, w_o = (bf(w) for w in _permute_q_heads(attn_w_qkv, attn_w_o))
    lru = (bf(lru_w_in), lru_conv_w, vecs(lru_conv_b), bf(lru_w_ra), vecs(lru_b_ra),
           bf(lru_w_rx), vecs(lru_b_rx), vecs(lru_lambda), bf(lru_w_out))
    g, b = vecs(ln_g), vecs(ln_b)

    h = x.reshape(batch * seq, D_MODEL)
    for i in range(DEPTH):
        h = _ffn_ln(h, *ffn_w[0], g, b, i, 3 * i)
        j = i // 2
        if i % 2 == 0:
            q, k, v = _qkv_rope(h, w_qkv, cos_t, sin_a, sin_b, seq, j)
            o = _swa_attn(attn_sinks[j], q, k, v, seq)
            h = _proj_ln(h, o, w_o, g, b, j, 3 * i + 1)
        else:
            h = _lru_ln(h, *lru, g, b, batch, seq, j, 3 * i + 1)
        h = _ffn_ln(h, *ffn_w[1], g, b, i, 3 * i + 2)
    return h.reshape(batch, seq, D_MODEL)
```

```python
import functools

import jax
import jax.numpy as jnp
from jax import lax
from jax.experimental import pallas as pl
from jax.experimental.pallas import tpu as pltpu

F32 = jnp.float32
BF16 = jnp.bfloat16

D_MODEL = 1024
DEPTH = 4
N_HEADS = 16
N_KV_HEADS = 4
HEAD_DIM = 64
GROUP = N_HEADS // N_KV_HEADS
WINDOW = 128
ROPE_THETA = 10000.0
D_RNN = 1024
RNN_BLOCKS = 4
RNN_BLOCK_W = D_RNN // RNN_BLOCKS
CONV_W = 4
LRU_C = 8.0
D_FF = 2816
ALPHA = (2.0 * DEPTH) ** 0.25
LN_EPS = 1e-5
QKV_COLS = (N_HEADS + 2 * N_KV_HEADS) * HEAD_DIM

LANES = 128
SUBLANES = 8
MIB = 1024 * 1024

TM_FFN = 1024
SUB_FFN = 512
FF_CHUNKS = (256,) * 11
assert sum(FF_CHUNKS) == D_FF
TM_PROJ = 1024
TQ_ATTN = 512
TM_LRU = 512

Q_HEAD_PERM = tuple((2 * kp + half) * GROUP + c
                    for kp in range(N_KV_HEADS // 2) for c in range(GROUP) for half in range(2))


def _resident(shape, layer=None):
    nd = len(shape)
    if layer is None:
        return pl.BlockSpec(tuple(shape), lambda *_: (0,) * nd, pipeline_mode=pl.Buffered(1))
    return pl.BlockSpec((None,) + tuple(shape), lambda *_: (layer,) + (0,) * nd,
                        pipeline_mode=pl.Buffered(1))


def _sigmoid(x):
    return 0.5 * jnp.tanh(0.5 * x) + 0.5


def _params(semantics, vmem_mib):
    return pltpu.CompilerParams(dimension_semantics=semantics, vmem_limit_bytes=vmem_mib * MIB)


def _layernorm(y, g, b):
    mu = jnp.mean(y, axis=-1, keepdims=True)
    yc = y - mu
    var = jnp.mean(yc * yc, axis=-1, keepdims=True)
    return yc * lax.rsqrt(var + LN_EPS) * g + b


def _ffn_kernel(x_ref, wi_ref, wo_ref, g_ref, b_ref, *rest, cast_next):
    if cast_next:
        nwi_ref, nwo_ref, o_ref, cwi_ref, cwo_ref, acc_ref = rest
        cwi_ref[...] = nwi_ref[...].astype(BF16)
        cwo_ref[...] = nwo_ref[...].astype(BF16)
    else:
        o_ref, acc_ref = rest
    for s in range(TM_FFN // SUB_FFN):
        rows = slice(s * SUB_FFN, (s + 1) * SUB_FFN)
        x = x_ref[rows, :]
        xb = x.astype(BF16)
        lo = 0
        for width in FF_CHUNKS:
            gate = jnp.dot(xb, wi_ref[:, lo:lo + width], preferred_element_type=F32)
            up = jnp.dot(xb, wi_ref[:, D_FF + lo:D_FF + lo + width], preferred_element_type=F32)
            act = (jax.nn.silu(gate) * up).astype(BF16)
            part = jnp.dot(act, wo_ref[lo:lo + width, :], preferred_element_type=F32)
            if lo == 0:
                acc_ref[rows, :] = part
            else:
                acc_ref[rows, :] += part
            lo += width
        y = ALPHA * x + 0.5 * acc_ref[rows, :]
        o_ref[rows, :] = _layernorm(y, g_ref[...], b_ref[...])


def _ffn_ln(x, wi, wo, ln_g, ln_b, ln_idx, next_w=None):
    t = x.shape[0]
    steps = t // TM_FFN
    tile = pl.BlockSpec((TM_FFN, D_MODEL), lambda i: (i, 0))
    in_specs = [tile, _resident((D_MODEL, 2 * D_FF)), _resident((D_FF, D_MODEL)),
                _resident((1, D_MODEL), ln_idx), _resident((1, D_MODEL), ln_idx)]
    out_specs, out_shape, args = [tile], [jax.ShapeDtypeStruct((t, D_MODEL), F32)], []
    if next_w is not None:
        nwi, nwo, nlayer = next_w
        ri, ro = D_MODEL // steps, D_FF // steps
        assert ri * steps == D_MODEL and ro * steps == D_FF and ri % 16 == 0 and ro % 16 == 0
        in_specs += [pl.BlockSpec((None, ri, 2 * D_FF), lambda i: (nlayer, i, 0)),
                     pl.BlockSpec((None, ro, D_MODEL), lambda i: (nlayer, i, 0))]
        out_specs += [pl.BlockSpec((ri, 2 * D_FF), lambda i: (i, 0)),
                      pl.BlockSpec((ro, D_MODEL), lambda i: (i, 0))]
        out_shape += [jax.ShapeDtypeStruct((D_MODEL, 2 * D_FF), BF16),
                      jax.ShapeDtypeStruct((D_FF, D_MODEL), BF16)]
        args = [nwi, nwo]
    return pl.pallas_call(
        functools.partial(_ffn_kernel, cast_next=next_w is not None),
        grid=(steps,),
        in_specs=in_specs,
        out_specs=out_specs,
        out_shape=out_shape,
        scratch_shapes=[pltpu.VMEM((TM_FFN, D_MODEL), F32)],
        compiler_params=_params(("parallel",), 56),
        name="ffn_ln",
    )(x, wi, wo, ln_g, ln_b, *args)


def _qkv_kernel(x_ref, w_ref, cos_ref, sa_ref, sb_ref, q_ref, k_ref, v_ref):
    z = jnp.dot(x_ref[...].astype(BF16), w_ref[...], preferred_element_type=F32)
    cos, sa, sb = cos_ref[...], sa_ref[...], sb_ref[...]

    def rope(t):
        return t * cos + pltpu.roll(t, LANES - 32, 1) * sa + pltpu.roll(t, 32, 1) * sb

    nq = N_HEADS * HEAD_DIM
    nk = N_KV_HEADS * HEAD_DIM
    for c in range(nq // LANES):
        sl = slice(c * LANES, (c + 1) * LANES)
        q_ref[:, sl] = (rope(z[:, sl]) * (HEAD_DIM ** -0.5)).astype(BF16)
    for c in range(nk // LANES):
        sl = slice(c * LANES, (c + 1) * LANES)
        k_ref[:, sl] = rope(z[:, nq + c * LANES:nq + (c + 1) * LANES]).astype(BF16)
    v_ref[...] = z[:, nq + nk:].astype(BF16)


def _qkv_rope(x, w, cos, sa, sb, seq, layer):
    t = x.shape[0]
    nq = N_HEADS * HEAD_DIM
    nk = N_KV_HEADS * HEAD_DIM
    tiles_per_seq = seq // TM_PROJ
    tab = pl.BlockSpec((TM_PROJ, LANES), lambda i: (i % tiles_per_seq, 0))
    row = lambda n: pl.BlockSpec((TM_PROJ, n), lambda i: (i, 0))
    return pl.pallas_call(
        _qkv_kernel,
        grid=(t // TM_PROJ,),
        in_specs=[row(D_MODEL), _resident((D_MODEL, QKV_COLS), layer), tab, tab, tab],
        out_specs=[row(nq), row(nk), row(nk)],
        out_shape=[jax.ShapeDtypeStruct((t, nq), BF16), jax.ShapeDtypeStruct((t, nk), BF16),
                   jax.ShapeDtypeStruct((t, nk), BF16)],
        compiler_params=_params(("parallel",), 32),
        name="qkv_rope",
    )(x, w, cos, sa, sb)


def _attn_kernel(sink_ref, q_ref, kc_ref, kp_ref, vc_ref, vp_ref, o_ref, *, tiles_per_seq):
    nblk = TQ_ATTN // WINDOW
    first = (pl.program_id(0) % tiles_per_seq) == 0
    kfull = jnp.concatenate([kp_ref[...], kc_ref[...]], axis=0)
    vfull = jnp.concatenate([vp_ref[...], vc_ref[...]], axis=0)

    qi = lax.broadcasted_iota(jnp.int32, (WINDOW, 2 * WINDOW), 0)
    kj = lax.broadcasted_iota(jnp.int32, (WINDOW, 2 * WINDOW), 1)
    band = (kj > qi) & (kj <= qi + WINDOW)
    band0 = band & (kj >= jnp.where(first, WINDOW, 0))
    col0 = kj[0:1, :] == 0
    lane_k = lax.broadcasted_iota(jnp.int32, (2 * WINDOW, LANES), 1)
    key0 = lax.broadcasted_iota(jnp.int32, (2 * WINDOW, LANES), 0) == 0
    lane_o = lax.broadcasted_iota(jnp.int32, (WINDOW, LANES), 1)
    ones = jnp.ones((2 * WINDOW, LANES), BF16)
    neg = jnp.finfo(F32).min

    for j in range(nblk):
        mask = band0 if j == 0 else band
        rows = slice(j * WINDOW, (j + 1) * WINDOW)
        keys = slice(j * WINDOW, (j + 2) * WINDOW)
        for kp in range(N_KV_HEADS // 2):
            pair = slice(kp * LANES, (kp + 1) * LANES)
            kblk = kfull[keys, pair]
            k_halves = (jnp.where(lane_k < HEAD_DIM, kblk, jnp.zeros_like(kblk)),
                        jnp.where(lane_k >= HEAD_DIM, kblk, jnp.zeros_like(kblk)))
            qstack = jnp.concatenate(
                [q_ref[rows, (kp * GROUP + c) * LANES:(kp * GROUP + c + 1) * LANES]
                 for c in range(GROUP)], axis=0)
            probs = []
            for half in range(2):
                s = lax.dot_general(qstack, k_halves[half], (((1,), (1,)), ((), ())),
                                    preferred_element_type=F32)
                for c in range(GROUP):
                    sink = sink_ref[(2 * kp + half) * GROUP + c]
                    fill = jnp.where(col0, sink, neg)
                    sc = jnp.where(mask, s[c * WINDOW:(c + 1) * WINDOW, :], fill)
                    m = jnp.max(sc, axis=-1, keepdims=True)
                    probs.append(jnp.exp(sc - m).astype(BF16))
            p = jnp.concatenate(probs, axis=0)
            vblk = vfull[keys, pair]
            vext = jnp.concatenate([jnp.where(key0, jnp.zeros_like(vblk), vblk), ones], axis=1)
            pv = jnp.dot(p, vext, preferred_element_type=F32)
            for c in range(GROUP):
                nums, dens = [], []
                for half in range(2):
                    r = (half * GROUP + c) * WINDOW
                    nums.append(pv[r:r + WINDOW, :LANES])
                    dens.append(pv[r:r + WINDOW, LANES:])
                low = lane_o < HEAD_DIM
                chunk = jnp.where(low, nums[0], nums[1]) / jnp.where(low, dens[0], dens[1])
                col = (kp * GROUP + c) * LANES
                o_ref[rows, col:col + LANES] = chunk.astype(BF16)


def _swa_attn(sinks, q, k, v, seq):
    t = q.shape[0]
    nq = N_HEADS * HEAD_DIM
    nk = N_KV_HEADS * HEAD_DIM
    blk_per_tile = TQ_ATTN // WINDOW
    cur = lambda n: pl.BlockSpec((TQ_ATTN, n), lambda i: (i, 0))
    prev = pl.BlockSpec((WINDOW, nk), lambda i: (jnp.maximum(i * blk_per_tile - 1, 0), 0))
    return pl.pallas_call(
        functools.partial(_attn_kernel, tiles_per_seq=seq // TQ_ATTN),
        grid=(t // TQ_ATTN,),
        in_specs=[pl.BlockSpec(memory_space=pltpu.SMEM), cur(nq), cur(nk), prev, cur(nk), prev],
        out_specs=cur(nq),
        out_shape=jax.ShapeDtypeStruct((t, nq), BF16),
        compiler_params=_params(("parallel",), 32),
        name="swa_attn",
    )(sinks, q, k, k, v, v)


def _proj_ln_kernel(res_ref, a_ref, w_ref, g_ref, b_ref, o_ref):
    y = ALPHA * res_ref[...] + jnp.dot(a_ref[...], w_ref[...], preferred_element_type=F32)
    o_ref[...] = _layernorm(y, g_ref[...], b_ref[...])


def _proj_ln(res, a, w, ln_g, ln_b, layer, ln_idx):
    t = res.shape[0]
    tile = pl.BlockSpec((TM_PROJ, D_MODEL), lambda i: (i, 0))
    return pl.pallas_call(
        _proj_ln_kernel,
        grid=(t // TM_PROJ,),
        in_specs=[tile, pl.BlockSpec((TM_PROJ, a.shape[1]), lambda i: (i, 0)),
                  _resident(w.shape[1:], layer),
                  _resident((1, D_MODEL), ln_idx), _resident((1, D_MODEL), ln_idx)],
        out_specs=tile,
        out_shape=jax.ShapeDtypeStruct((t, D_MODEL), F32),
        compiler_params=_params(("parallel",), 32),
        name="proj_ln",
    )(res, a, w, ln_g, ln_b)


def _lru_kernel(x_ref, win_ref, cw_ref, cb_ref, wra_ref, bra_ref, wrx_ref, brx_ref, lam_ref,
                wout_ref, g_ref, b_ref, o_ref, xpad_ref, a_ref, bb_ref, h_ref, carry_ref):
    tm = TM_LRU

    @pl.when(pl.program_id(1) == 0)
    def _():
        xpad_ref[...] = jnp.zeros((SUBLANES, D_RNN), F32)
        carry_ref[...] = jnp.zeros((1, D_RNN), F32)

    x = x_ref[...]
    z = jnp.dot(x.astype(BF16), win_ref[...], preferred_element_type=F32)
    xb = z[:, :D_RNN]
    gate = jax.nn.gelu(z[:, D_RNN:])

    cw = cw_ref[...]
    groups = jnp.concatenate([xpad_ref[...], xb], axis=0).reshape(tm // SUBLANES + 1, SUBLANES, D_RNN)
    sub = lax.broadcasted_iota(jnp.int32, (1, SUBLANES, D_RNN), 1)
    xc = cb_ref[...] + cw[CONV_W - 1:CONV_W, :] * xb
    for d in range(1, CONV_W):
        rot = pltpu.roll(groups, d, 1)
        shifted = jnp.where(sub >= d, rot[1:], rot[:-1]).reshape(tm, D_RNN)
        xc = xc + cw[CONV_W - 1 - d:CONV_W - d, :] * shifted
    xpad_ref[...] = xb[tm - SUBLANES:, :]

    lam = lam_ref[...]
    c_log_sig = LRU_C * (jnp.minimum(lam, 0.0) - jnp.log1p(jnp.exp(-jnp.abs(lam))))
    for n in range(RNN_BLOCKS):
        sl = slice(n * RNN_BLOCK_W, (n + 1) * RNN_BLOCK_W)
        xn = xc[:, sl]
        xr = xn.astype(BF16)
        r = _sigmoid(jnp.dot(xr, wra_ref[n], preferred_element_type=F32) + bra_ref[:, sl])
        i = _sigmoid(jnp.dot(xr, wrx_ref[n], preferred_element_type=F32) + brx_ref[:, sl])
        log_a = r * c_log_sig[:, sl]
        a = jnp.exp(log_a)
        a_ref[:, sl] = a
        q = -jnp.tanh(log_a) * (a * a + 1.0)
        root = jnp.where(q == 0.0, 0.0, q * lax.rsqrt(q))
        bb_ref[:, sl] = root * (i * xn)

    row = lax.broadcasted_iota(jnp.int32, (SUBLANES, D_RNN), 0)

    def group(gi, h_in):
        r0 = pl.multiple_of(gi * SUBLANES, SUBLANES)
        a = a_ref[pl.ds(r0, SUBLANES), :]
        b = bb_ref[pl.ds(r0, SUBLANES), :]
        for d in (1, 2, 4):
            a_sh = jnp.where(row >= d, pltpu.roll(a, d, 0), 1.0)
            b_sh = jnp.where(row >= d, pltpu.roll(b, d, 0), 0.0)
            b = a * b_sh + b
            a = a * a_sh
        h = a * h_in + b
        h_ref[pl.ds(r0, SUBLANES), :] = h
        return h[SUBLANES - 1:SUBLANES, :]

    carry_ref[...] = lax.fori_loop(0, tm // SUBLANES, group, carry_ref[...])

    y = (h_ref[...] * gate).astype(BF16)
    out = ALPHA * x + jnp.dot(y, wout_ref[...], preferred_element_type=F32)
    o_ref[...] = _layernorm(out, g_ref[...], b_ref[...])


def _lru_ln(x, win, cw, cb, wra, bra, wrx, brx, lam, wout, ln_g, ln_b, batch, seq, layer, ln_idx):
    tiles = seq // TM_LRU
    tile = pl.BlockSpec((TM_LRU, D_MODEL), lambda bi, ti: (bi * tiles + ti, 0))
    vec = _resident((1, D_RNN), layer)
    gates = _resident((RNN_BLOCKS, RNN_BLOCK_W, RNN_BLOCK_W), layer)
    return pl.pallas_call(
        _lru_kernel,
        grid=(batch, tiles),
        in_specs=[tile, _resident((D_MODEL, 2 * D_RNN), layer), _resident((CONV_W, D_RNN), layer),
                  vec, gates, vec, gates, vec, vec, _resident((D_RNN, D_MODEL), layer),
                  _resident((1, D_MODEL), ln_idx), _resident((1, D_MODEL), ln_idx)],
        out_specs=tile,
        out_shape=jax.ShapeDtypeStruct((batch * seq, D_MODEL), F32),
        scratch_shapes=[pltpu.VMEM((SUBLANES, D_RNN), F32),
                        pltpu.VMEM((TM_LRU, D_RNN), F32),
                        pltpu.VMEM((TM_LRU, D_RNN), F32),
                        pltpu.VMEM((TM_LRU, D_RNN), F32),
                        pltpu.VMEM((1, D_RNN), F32)],
        compiler_params=_params(("arbitrary", "arbitrary"), 48),
        name="lru_ln",
    )(x, win, cw, cb, wra, bra, wrx, brx, lam, wout, ln_g, ln_b)


def _rope_tables(seq):
    pos = jnp.arange(seq, dtype=F32)
    inv_freq = ROPE_THETA ** (-jnp.arange(0, HEAD_DIM, 2, dtype=F32) / HEAD_DIM)
    ang = pos[:, None] * inv_freq[None, :]
    cos, sin = jnp.cos(ang), jnp.sin(ang)
    zero = jnp.zeros_like(sin)
    cos_t = jnp.concatenate([cos, cos, cos, cos], axis=-1)
    sin_a = jnp.concatenate([-sin, zero, -sin, zero], axis=-1)
    sin_b = jnp.concatenate([zero, sin, zero, sin], axis=-1)
    return cos_t, sin_a, sin_b


def _permute_q_heads(w_qkv, w_o):
    nq = N_HEADS * HEAD_DIM
    n = w_qkv.shape[0]
    perm = jnp.asarray(Q_HEAD_PERM)
    wq = w_qkv[:, :, :nq].reshape(n, D_MODEL, N_HEADS, HEAD_DIM)[:, :, perm, :]
    w_qkv_p = jnp.concatenate([wq.reshape(n, D_MODEL, nq), w_qkv[:, :, nq:]], axis=2)
    w_o_p = w_o.reshape(n, N_HEADS, HEAD_DIM, D_MODEL)[:, perm].reshape(n, nq, D_MODEL)
    return w_qkv_p, w_o_p


def kernel(x, ffn1_w_in, ffn1_w_out, ffn2_w_in, ffn2_w_out, ln_g, ln_b, attn_w_qkv, attn_sinks,
           attn_w_o, lru_w_in, lru_conv_w, lru_conv_b, lru_w_ra, lru_b_ra, lru_w_rx, lru_b_rx,
           lru_lambda, lru_w_out):
    batch, seq, _ = x.shape
    cos_t, sin_a, sin_b = _rope_tables(seq)
    bf = lambda a: a.astype(BF16)
    vecs = lambda a: a.reshape(-1, 1, a.shape[-1])
    w_qkv, w_o = (bf(w) for w in _permute_q_heads(attn_w_qkv, attn_w_o))
    lru = (bf(lru_w_in), lru_conv_w, vecs(lru_conv_b), bf(lru_w_ra), vecs(lru_b_ra),
           bf(lru_w_rx), vecs(lru_b_rx), vecs(lru_lambda), bf(lru_w_out))
    g, b = vecs(ln_g), vecs(ln_b)

    h = x.reshape(batch * seq, D_MODEL)
    wi, wo = bf(ffn1_w_in[0]), bf(ffn1_w_out[0])
    for i in range(DEPTH):
        h, wi, wo = _ffn_ln(h, wi, wo, g, b, 3 * i, next_w=(ffn2_w_in, ffn2_w_out, i))
        j = i // 2
        if i % 2 == 0:
            q, k, v = _qkv_rope(h, w_qkv, cos_t, sin_a, sin_b, seq, j)
            o = _swa_attn(attn_sinks[j], q, k, v, seq)
            h = _proj_ln(h, o, w_o, g, b, j, 3 * i + 1)
        else:
            h = _lru_ln(h, *lru, g, b, batch, seq, j, 3 * i + 1)
        if i + 1 < DEPTH:
            h, wi, wo = _ffn_ln(h, wi, wo, g, b, 3 * i + 2, next_w=(ffn1_w_in, ffn1_w_out, i + 1))
        else:
            (h,) = _ffn_ln(h, wi, wo, g, b, 3 * i + 2)
    return h.reshape(batch, seq, D_MODEL)
```

```python
import functools

import jax
import jax.numpy as jnp
from jax import lax
from jax.experimental import pallas as pl
from jax.experimental.pallas import tpu as pltpu

F32 = jnp.float32
BF16 = jnp.bfloat16

D_MODEL = 1024
DEPTH = 4
N_HEADS = 16
N_KV_HEADS = 4
HEAD_DIM = 64
GROUP = N_HEADS // N_KV_HEADS
WINDOW = 128
ROPE_THETA = 10000.0
D_RNN = 1024
RNN_BLOCKS = 4
RNN_BLOCK_W = D_RNN // RNN_BLOCKS
CONV_W = 4
LRU_C = 8.0
D_FF = 2816
ALPHA = (2.0 * DEPTH) ** 0.25
LN_EPS = 1e-5
QKV_COLS = (N_HEADS + 2 * N_KV_HEADS) * HEAD_DIM

LANES = 128
SUBLANES = 8
MIB = 1024 * 1024

TM_FFN = 1024
SUB_FFN = 512
FF_CHUNKS = (256,) * 11
assert sum(FF_CHUNKS) == D_FF
TM_PROJ = 1024
TQ_ATTN = 512
TM_LRU = 512

Q_HEAD_PERM = tuple((2 * kp + half) * GROUP + c
                    for kp in range(N_KV_HEADS // 2) for c in range(GROUP) for half in range(2))


def _resident(shape, layer=None):
    nd = len(shape)
    if layer is None:
        return pl.BlockSpec(tuple(shape), lambda *_: (0,) * nd, pipeline_mode=pl.Buffered(1))
    return pl.BlockSpec((None,) + tuple(shape), lambda *_: (layer,) + (0,) * nd,
                        pipeline_mode=pl.Buffered(1))


def _sigmoid(x):
    return 0.5 * jnp.tanh(0.5 * x) + 0.5


def _params(semantics, vmem_mib):
    return pltpu.CompilerParams(dimension_semantics=semantics, vmem_limit_bytes=vmem_mib * MIB)


def _layernorm(y, g, b):
    mu = jnp.mean(y, axis=-1, keepdims=True)
    yc = y - mu
    var = jnp.mean(yc * yc, axis=-1, keepdims=True)
    return yc * lax.rsqrt(var + LN_EPS) * g + b


def _ffn_rows(x_ref, rows, wi_ref, wo_ref, g_ref, b_ref, o_ref, acc_ref, interleave=None):
    xb = x_ref[rows, :].astype(BF16)
    lo = 0
    for width in FF_CHUNKS:
        if interleave is not None:
            next(interleave, None)
        gate = jnp.dot(xb, wi_ref[:, lo:lo + width], preferred_element_type=F32)
        up = jnp.dot(xb, wi_ref[:, D_FF + lo:D_FF + lo + width], preferred_element_type=F32)
        half = 0.5 * gate
        act = ((half * jnp.tanh(half) + half) * up).astype(BF16)
        part = jnp.dot(act, wo_ref[lo:lo + width, :], preferred_element_type=F32)
        if lo == 0:
            acc_ref[rows, :] = part
        else:
            acc_ref[rows, :] += part
        lo += width
    y = ALPHA * x_ref[rows, :] + 0.5 * acc_ref[rows, :]
    o_ref[rows, :] = _layernorm(y, g_ref[...], b_ref[...])


def _cast_slab(src_refs, dst_refs):
    for src, dst in zip(src_refs, dst_refs):
        dst[...] = src[...].astype(BF16)


def _ffn_kernel(x_ref, wi_ref, wo_ref, g_ref, b_ref, *rest, cast_next):
    if cast_next:
        nwi_ref, nwo_ref, o_ref, cwi_ref, cwo_ref, acc_ref = rest
        _cast_slab((nwi_ref, nwo_ref), (cwi_ref, cwo_ref))
    else:
        o_ref, acc_ref = rest
    for s in range(TM_FFN // SUB_FFN):
        rows = slice(s * SUB_FFN, (s + 1) * SUB_FFN)
        _ffn_rows(x_ref, rows, wi_ref, wo_ref, g_ref, b_ref, o_ref, acc_ref)


def _cast_specs(layer, slabs):
    ri, ro = D_MODEL // slabs, D_FF // slabs
    assert ri * slabs == D_MODEL and ro * slabs == D_FF and ri % 16 == 0 and ro % 16 == 0
    slab = lambda i: jnp.minimum(i, slabs - 1)
    in_specs = [pl.BlockSpec((None, ri, 2 * D_FF), lambda i: (layer, slab(i), 0)),
                pl.BlockSpec((None, ro, D_MODEL), lambda i: (layer, slab(i), 0))]
    out_specs = [pl.BlockSpec((ri, 2 * D_FF), lambda i: (slab(i), 0)),
                 pl.BlockSpec((ro, D_MODEL), lambda i: (slab(i), 0))]
    out_shape = [jax.ShapeDtypeStruct((D_MODEL, 2 * D_FF), BF16),
                 jax.ShapeDtypeStruct((D_FF, D_MODEL), BF16)]
    return in_specs, out_specs, out_shape


def _ffn_ln(x, wi, wo, ln_g, ln_b, ln_idx, next_w=None):
    t = x.shape[0]
    steps = t // TM_FFN
    tile = pl.BlockSpec((TM_FFN, D_MODEL), lambda i: (i, 0))
    in_specs = [tile, _resident((D_MODEL, 2 * D_FF)), _resident((D_FF, D_MODEL)),
                _resident((1, D_MODEL), ln_idx), _resident((1, D_MODEL), ln_idx)]
    out_specs, out_shape, args = [tile], [jax.ShapeDtypeStruct((t, D_MODEL), F32)], []
    if next_w is not None:
        cast_in, cast_out, cast_shape = _cast_specs(next_w[2], steps)
        in_specs += cast_in
        out_specs += cast_out
        out_shape += cast_shape
        args = list(next_w[:2])
    return pl.pallas_call(
        functools.partial(_ffn_kernel, cast_next=next_w is not None),
        grid=(steps,),
        in_specs=in_specs,
        out_specs=out_specs,
        out_shape=out_shape,
        scratch_shapes=[pltpu.VMEM((TM_FFN, D_MODEL), F32)],
        compiler_params=_params(("parallel",), 56),
        name="ffn_ln",
    )(x, wi, wo, ln_g, ln_b, *args)


def _qkv_kernel(x_ref, w_ref, cos_ref, sa_ref, sb_ref, q_ref, k_ref, v_ref):
    z = jnp.dot(x_ref[...].astype(BF16), w_ref[...], preferred_element_type=F32)
    cos, sa, sb = cos_ref[...], sa_ref[...], sb_ref[...]

    def rope(t):
        return t * cos + pltpu.roll(t, LANES - 32, 1) * sa + pltpu.roll(t, 32, 1) * sb

    nq = N_HEADS * HEAD_DIM
    nk = N_KV_HEADS * HEAD_DIM
    for c in range(nq // LANES):
        sl = slice(c * LANES, (c + 1) * LANES)
        q_ref[:, sl] = (rope(z[:, sl]) * (HEAD_DIM ** -0.5)).astype(BF16)
    for c in range(nk // LANES):
        sl = slice(c * LANES, (c + 1) * LANES)
        k_ref[:, sl] = rope(z[:, nq + c * LANES:nq + (c + 1) * LANES]).astype(BF16)
    v_ref[...] = z[:, nq + nk:].astype(BF16)


def _qkv_rope(x, w, cos, sa, sb, seq, layer):
    t = x.shape[0]
    nq = N_HEADS * HEAD_DIM
    nk = N_KV_HEADS * HEAD_DIM
    tiles_per_seq = seq // TM_PROJ
    tab = pl.BlockSpec((TM_PROJ, LANES), lambda i: (i % tiles_per_seq, 0))
    row = lambda n: pl.BlockSpec((TM_PROJ, n), lambda i: (i, 0))
    return pl.pallas_call(
        _qkv_kernel,
        grid=(t // TM_PROJ,),
        in_specs=[row(D_MODEL), _resident((D_MODEL, QKV_COLS), layer), tab, tab, tab],
        out_specs=[row(nq), row(nk), row(nk)],
        out_shape=[jax.ShapeDtypeStruct((t, nq), BF16), jax.ShapeDtypeStruct((t, nk), BF16),
                   jax.ShapeDtypeStruct((t, nk), BF16)],
        compiler_params=_params(("parallel",), 32),
        name="qkv_rope",
    )(x, w, cos, sa, sb)


def _attn_kernel(sink_ref, q_ref, kc_ref, kp_ref, vc_ref, vp_ref, o_ref, *, tiles_per_seq):
    nblk = TQ_ATTN // WINDOW
    first = (pl.program_id(0) % tiles_per_seq) == 0
    kfull = jnp.concatenate([kp_ref[...], kc_ref[...]], axis=0)
    vfull = jnp.concatenate([vp_ref[...], vc_ref[...]], axis=0)

    qi = lax.broadcasted_iota(jnp.int32, (WINDOW, 2 * WINDOW), 0)
    kj = lax.broadcasted_iota(jnp.int32, (WINDOW, 2 * WINDOW), 1)
    band = (kj > qi) & (kj <= qi + WINDOW)
    band0 = band & (kj >= jnp.where(first, WINDOW, 0))
    col0 = kj[0:1, :] == 0
    lane_k = lax.broadcasted_iota(jnp.int32, (2 * WINDOW, LANES), 1)
    key0 = lax.broadcasted_iota(jnp.int32, (2 * WINDOW, LANES), 0) == 0
    lane_o = lax.broadcasted_iota(jnp.int32, (WINDOW, LANES), 1)
    ones = jnp.ones((2 * WINDOW, LANES), BF16)
    neg = jnp.finfo(F32).min

    for j in range(nblk):
        mask = band0 if j == 0 else band
        rows = slice(j * WINDOW, (j + 1) * WINDOW)
        keys = slice(j * WINDOW, (j + 2) * WINDOW)
        for kp in range(N_KV_HEADS // 2):
            pair = slice(kp * LANES, (kp + 1) * LANES)
            kblk = kfull[keys, pair]
            k_halves = (jnp.where(lane_k < HEAD_DIM, kblk, jnp.zeros_like(kblk)),
                        jnp.where(lane_k >= HEAD_DIM, kblk, jnp.zeros_like(kblk)))
            qstack = jnp.concatenate(
                [q_ref[rows, (kp * GROUP + c) * LANES:(kp * GROUP + c + 1) * LANES]
                 for c in range(GROUP)], axis=0)
            probs = []
            for half in range(2):
                s = lax.dot_general(qstack, k_halves[half], (((1,), (1,)), ((), ())),
                                    preferred_element_type=F32)
                for c in range(GROUP):
                    sink = sink_ref[(2 * kp + half) * GROUP + c]
                    fill = jnp.where(col0, sink, neg)
                    sc = jnp.where(mask, s[c * WINDOW:(c + 1) * WINDOW, :], fill)
                    m = jnp.max(sc, axis=-1, keepdims=True)
                    probs.append(jnp.exp(sc - m).astype(BF16))
            p = jnp.concatenate(probs, axis=0)
            vblk = vfull[keys, pair]
            vext = jnp.concatenate([jnp.where(key0, jnp.zeros_like(vblk), vblk), ones], axis=1)
            pv = jnp.dot(p, vext, preferred_element_type=F32)
            for c in range(GROUP):
                nums, dens = [], []
                for half in range(2):
                    r = (half * GROUP + c) * WINDOW
                    nums.append(pv[r:r + WINDOW, :LANES])
                    dens.append(pv[r:r + WINDOW, LANES:])
                low = lane_o < HEAD_DIM
                chunk = jnp.where(low, nums[0], nums[1]) / jnp.where(low, dens[0], dens[1])
                col = (kp * GROUP + c) * LANES
                o_ref[rows, col:col + LANES] = chunk.astype(BF16)


def _swa_attn(sinks, q, k, v, seq):
    t = q.shape[0]
    nq = N_HEADS * HEAD_DIM
    nk = N_KV_HEADS * HEAD_DIM
    blk_per_tile = TQ_ATTN // WINDOW
    cur = lambda n: pl.BlockSpec((TQ_ATTN, n), lambda i: (i, 0))
    prev = pl.BlockSpec((WINDOW, nk), lambda i: (jnp.maximum(i * blk_per_tile - 1, 0), 0))
    return pl.pallas_call(
        functools.partial(_attn_kernel, tiles_per_seq=seq // TQ_ATTN),
        grid=(t // TQ_ATTN,),
        in_specs=[pl.BlockSpec(memory_space=pltpu.SMEM), cur(nq), cur(nk), prev, cur(nk), prev],
        out_specs=cur(nq),
        out_shape=jax.ShapeDtypeStruct((t, nq), BF16),
        compiler_params=_params(("parallel",), 32),
        name="swa_attn",
    )(sinks, q, k, k, v, v)


def _proj_ln_kernel(res_ref, a_ref, w_ref, g_ref, b_ref, o_ref):
    y = ALPHA * res_ref[...] + jnp.dot(a_ref[...], w_ref[...], preferred_element_type=F32)
    o_ref[...] = _layernorm(y, g_ref[...], b_ref[...])


def _proj_ln(res, a, w, ln_g, ln_b, layer, ln_idx):
    t = res.shape[0]
    tile = pl.BlockSpec((TM_PROJ, D_MODEL), lambda i: (i, 0))
    return pl.pallas_call(
        _proj_ln_kernel,
        grid=(t // TM_PROJ,),
        in_specs=[tile, pl.BlockSpec((TM_PROJ, a.shape[1]), lambda i: (i, 0)),
                  _resident(w.shape[1:], layer),
                  _resident((1, D_MODEL), ln_idx), _resident((1, D_MODEL), ln_idx)],
        out_specs=tile,
        out_shape=jax.ShapeDtypeStruct((t, D_MODEL), F32),
        compiler_params=_params(("parallel",), 32),
        name="proj_ln",
    )(res, a, w, ln_g, ln_b)


def _lru_pieces(x_ref, win_ref, cw_ref, cb_ref, wra_ref, bra_ref, wrx_ref, brx_ref, lam_ref,
                wout_ref, g_ref, b_ref, out_ref, z_ref, hb_ref, xpad_ref, carry_ref):
    tm = TM_LRU
    half = D_RNN // 2
    z_ref[...] = jnp.dot(x_ref[...].astype(BF16), win_ref[...], preferred_element_type=F32)
    yield
    z_ref[:, D_RNN:] = jax.nn.gelu(z_ref[:, D_RNN:])
    yield

    for lanes in (slice(0, half), slice(half, D_RNN)):
        sub = lax.broadcasted_iota(jnp.int32, (1, SUBLANES, half), 1)
        cw = cw_ref[:, lanes]
        xb = z_ref[:, lanes]
        groups = jnp.concatenate([xpad_ref[:, lanes], xb], axis=0)
        groups = groups.reshape(tm // SUBLANES + 1, SUBLANES, half)
        xc = cb_ref[:, lanes] + cw[CONV_W - 1:CONV_W, :] * xb
        for d in range(1, CONV_W):
            rot = pltpu.roll(groups, d, 1)
            shifted = jnp.where(sub >= d, rot[1:], rot[:-1]).reshape(tm, half)
            xc = xc + cw[CONV_W - 1 - d:CONV_W - d, :] * shifted
        hb_ref[:, lanes] = xc
        xpad_ref[:, lanes] = xb[tm - SUBLANES:, :]
        yield

    for n in range(RNN_BLOCKS):
        sl = slice(n * RNN_BLOCK_W, (n + 1) * RNN_BLOCK_W)
        lam = lam_ref[:, sl]
        c_log_sig = LRU_C * (jnp.minimum(lam, 0.0) - jnp.log1p(jnp.exp(-jnp.abs(lam))))
        xn = hb_ref[:, sl]
        xr = xn.astype(BF16)
        r = _sigmoid(jnp.dot(xr, wra_ref[n], preferred_element_type=F32) + bra_ref[:, sl])
        i = _sigmoid(jnp.dot(xr, wrx_ref[n], preferred_element_type=F32) + brx_ref[:, sl])
        log_a = r * c_log_sig
        a = jnp.exp(log_a)
        z_ref[:, sl] = a
        q = -jnp.tanh(log_a) * (a * a + 1.0)
        root = jnp.where(q == 0.0, 0.0, q * lax.rsqrt(q))
        hb_ref[:, sl] = root * (i * xn)
        if n % 2 == 1:
            yield

    groups_per_piece = tm // SUBLANES // LRU_SCAN_PIECES
    for piece in range(LRU_SCAN_PIECES):
        row = lax.broadcasted_iota(jnp.int32, (SUBLANES, D_RNN), 0)
        h_in = carry_ref[...]
        for gi in range(piece * groups_per_piece, (piece + 1) * groups_per_piece):
            rows = slice(gi * SUBLANES, (gi + 1) * SUBLANES)
            a = z_ref[rows, :D_RNN]
            b = hb_ref[rows, :]
            for d in (1, 2, 4):
                a_sh = jnp.where(row >= d, pltpu.roll(a, d, 0), 1.0)
                b_sh = jnp.where(row >= d, pltpu.roll(b, d, 0), 0.0)
                b = a * b_sh + b
                a = a * a_sh
            h = a * h_in + b
            hb_ref[rows, :] = h
            h_in = h[SUBLANES - 1:SUBLANES, :]
        carry_ref[...] = h_in
        yield

    y = (hb_ref[...] * z_ref[:, D_RNN:]).astype(BF16)
    z_ref[:, :D_RNN] = ALPHA * x_ref[...] + jnp.dot(y, wout_ref[...], preferred_element_type=F32)
    yield
    out_ref[...] = _layernorm(z_ref[:, :D_RNN], g_ref[...], b_ref[...])


LRU_SCAN_PIECES = 4

N_LRU_REFS = 12


def _lru_ffn_kernel(*refs, tiles_per_seq, cast_next):
    lru_in, refs = refs[:N_LRU_REFS], refs[N_LRU_REFS:]
    wi_ref, wo_ref, g2_ref, b2_ref = refs[:4]
    if cast_next:
        nwi_ref, nwo_ref, o_ref, cwi_ref, cwo_ref = refs[4:9]
        scratch = refs[9:]
        _cast_slab((nwi_ref, nwo_ref), (cwi_ref, cwo_ref))
    else:
        o_ref = refs[4]
        scratch = refs[5:]
    mid_ref, xin_ref, acc_ref, z_ref, hb_ref, xpad_ref, carry_ref = scratch
    step = pl.program_id(0)

    @pl.when(step == 0)
    def _():
        mid_ref[...] = jnp.zeros((TM_LRU, D_MODEL), F32)

    @pl.when(step % tiles_per_seq == 0)
    def _():
        xpad_ref[...] = jnp.zeros((SUBLANES, D_RNN), F32)
        carry_ref[...] = jnp.zeros((1, D_RNN), F32)

    xin_ref[...] = mid_ref[...]
    pieces = _lru_pieces(*lru_in, mid_ref, z_ref, hb_ref, xpad_ref, carry_ref)
    _ffn_rows(xin_ref, slice(0, TM_LRU), wi_ref, wo_ref, g2_ref, b2_ref, o_ref, acc_ref,
              interleave=pieces)
    for _ in pieces:
        pass


def _lru_ffn(x, lru_w, ln_g, ln_b, wi, wo, seq, layer, ln_lru, ln_ffn, next_w=None):
    t = x.shape[0]
    tiles = t // TM_LRU
    tile_in = pl.BlockSpec((TM_LRU, D_MODEL), lambda i: (jnp.minimum(i, tiles - 1), 0))
    tile_out = pl.BlockSpec((TM_LRU, D_MODEL), lambda i: (jnp.maximum(i - 1, 0), 0))
    vec = _resident((1, D_RNN), layer)
    gates = _resident((RNN_BLOCKS, RNN_BLOCK_W, RNN_BLOCK_W), layer)
    in_specs = [tile_in, _resident((D_MODEL, 2 * D_RNN), layer), _resident((CONV_W, D_RNN), layer),
                vec, gates, vec, gates, vec, vec, _resident((D_RNN, D_MODEL), layer),
                _resident((1, D_MODEL), ln_lru), _resident((1, D_MODEL), ln_lru),
                _resident((D_MODEL, 2 * D_FF)), _resident((D_FF, D_MODEL)),
                _resident((1, D_MODEL), ln_ffn), _resident((1, D_MODEL), ln_ffn)]
    assert len(in_specs) == N_LRU_REFS + 4
    out_specs, out_shape, args = [tile_out], [jax.ShapeDtypeStruct((t, D_MODEL), F32)], []
    if next_w is not None:
        cast_in, cast_out, cast_shape = _cast_specs(next_w[2], 16)
        in_specs += cast_in
        out_specs += cast_out
        out_shape += cast_shape
        args = list(next_w[:2])
    return pl.pallas_call(
        functools.partial(_lru_ffn_kernel, tiles_per_seq=seq // TM_LRU,
                          cast_next=next_w is not None),
        grid=(tiles + 1,),
        in_specs=in_specs,
        out_specs=out_specs,
        out_shape=out_shape,
        scratch_shapes=[pltpu.VMEM((TM_LRU, D_MODEL), F32),
                        pltpu.VMEM((TM_LRU, D_MODEL), F32),
                        pltpu.VMEM((TM_LRU, D_MODEL), F32),
                        pltpu.VMEM((TM_LRU, 2 * D_RNN), F32),
                        pltpu.VMEM((TM_LRU, D_RNN), F32),
                        pltpu.VMEM((SUBLANES, D_RNN), F32),
                        pltpu.VMEM((1, D_RNN), F32)],
        compiler_params=_params(("arbitrary",), 58),
        name="lru_ffn",
    )(x, *lru_w, ln_g, ln_b, wi, wo, ln_g, ln_b, *args)


def _rope_tables(seq):
    pos = jnp.arange(seq, dtype=F32)
    inv_freq = ROPE_THETA ** (-jnp.arange(0, HEAD_DIM, 2, dtype=F32) / HEAD_DIM)
    ang = pos[:, None] * inv_freq[None, :]
    cos, sin = jnp.cos(ang), jnp.sin(ang)
    zero = jnp.zeros_like(sin)
    cos_t = jnp.concatenate([cos, cos, cos, cos], axis=-1)
    sin_a = jnp.concatenate([-sin, zero, -sin, zero], axis=-1)
    sin_b = jnp.concatenate([zero, sin, zero, sin], axis=-1)
    return cos_t, sin_a, sin_b


def _permute_q_heads(w_qkv, w_o):
    nq = N_HEADS * HEAD_DIM
    n = w_qkv.shape[0]
    perm = jnp.asarray(Q_HEAD_PERM)
    wq = w_qkv[:, :, :nq].reshape(n, D_MODEL, N_HEADS, HEAD_DIM)[:, :, perm, :]
    w_qkv_p = jnp.concatenate([wq.reshape(n, D_MODEL, nq), w_qkv[:, :, nq:]], axis=2)
    w_o_p = w_o.reshape(n, N_HEADS, HEAD_DIM, D_MODEL)[:, perm].reshape(n, nq, D_MODEL)
    return w_qkv_p, w_o_p


def kernel(x, ffn1_w_in, ffn1_w_out, ffn2_w_in, ffn2_w_out, ln_g, ln_b, attn_w_qkv, attn_sinks,
           attn_w_o, lru_w_in, lru_conv_w, lru_conv_b, lru_w_ra, lru_b_ra, lru_w_rx, lru_b_rx,
           lru_lambda, lru_w_out):
    batch, seq, _ = x.shape
    cos_t, sin_a, sin_b = _rope_tables(seq)
    bf = lambda a: a.astype(BF16)
    vecs = lambda a: a.reshape(-1, 1, a.shape[-1])
    w_qkv, w_o = (bf(w) for w in _permute_q_heads(attn_w_qkv, attn_w_o))
    lru = (bf(lru_w_in), lru_conv_w, vecs(lru_conv_b), bf(lru_w_ra), vecs(lru_b_ra),
           bf(lru_w_rx), vecs(lru_b_rx), vecs(lru_lambda), bf(lru_w_out))
    g, b = vecs(ln_g), vecs(ln_b)

    h = x.reshape(batch * seq, D_MODEL)
    wi, wo = bf(ffn1_w_in[0]), bf(ffn1_w_out[0])
    for i in range(DEPTH):
        h, wi, wo = _ffn_ln(h, wi, wo, g, b, 3 * i, next_w=(ffn2_w_in, ffn2_w_out, i))
        j = i // 2
        next_w = (ffn1_w_in, ffn1_w_out, i + 1) if i + 1 < DEPTH else None
        if i % 2 == 0:
            q, k, v = _qkv_rope(h, w_qkv, cos_t, sin_a, sin_b, seq, j)
            o = _swa_attn(attn_sinks[j], q, k, v, seq)
            h = _proj_ln(h, o, w_o, g, b, j, 3 * i + 1)
            outs = _ffn_ln(h, wi, wo, g, b, 3 * i + 2, next_w=next_w)
        else:
            outs = _lru_ffn(h, lru, g, b, wi, wo, seq, j, 3 * i + 1, 3 * i + 2, next_w=next_w)
        h = outs[0]
        if next_w is not None:
            wi, wo = outs[1:]
    return h.reshape(batch, seq, D_MODEL)
```

```python
import functools

import jax
import jax.numpy as jnp
from jax import lax
from jax.experimental import pallas as pl
from jax.experimental.pallas import tpu as pltpu

F32 = jnp.float32
BF16 = jnp.bfloat16

D_MODEL = 1024
DEPTH = 4
N_HEADS = 16
N_KV_HEADS = 4
HEAD_DIM = 64
GROUP = N_HEADS // N_KV_HEADS
WINDOW = 128
ROPE_THETA = 10000.0
D_RNN = 1024
RNN_BLOCKS = 4
RNN_BLOCK_W = D_RNN // RNN_BLOCKS
CONV_W = 4
LRU_C = 8.0
D_FF = 2816
ALPHA = (2.0 * DEPTH) ** 0.25
LN_EPS = 1e-5
QKV_COLS = (N_HEADS + 2 * N_KV_HEADS) * HEAD_DIM

LANES = 128
SUBLANES = 8
MIB = 1024 * 1024

TM_FFN = 1024
SUB_FFN = 512
FF_CHUNKS = (256,) * 11
assert sum(FF_CHUNKS) == D_FF
TM_PROJ = 1024
ROPE_SPLIT = 64
TQ_ATTN = 512
TM_LRU = 512

Q_HEAD_PERM = tuple((2 * kp + half) * GROUP + c
                    for kp in range(N_KV_HEADS // 2) for c in range(GROUP) for half in range(2))


def _resident(shape, layer=None):
    nd = len(shape)
    if layer is None:
        return pl.BlockSpec(tuple(shape), lambda *_: (0,) * nd, pipeline_mode=pl.Buffered(1))
    return pl.BlockSpec((None,) + tuple(shape), lambda *_: (layer,) + (0,) * nd,
                        pipeline_mode=pl.Buffered(1))


def _sigmoid(x):
    return 0.5 * jnp.tanh(0.5 * x) + 0.5


def _params(semantics, vmem_mib):
    return pltpu.CompilerParams(dimension_semantics=semantics, vmem_limit_bytes=vmem_mib * MIB)


def _layernorm(y, g, b):
    mu = jnp.mean(y, axis=-1, keepdims=True)
    yc = y - mu
    var = jnp.mean(yc * yc, axis=-1, keepdims=True)
    return yc * lax.rsqrt(var + LN_EPS) * g + b


def _ffn_rows(x_ref, rows, wi_ref, wo_ref, g_ref, b_ref, o_ref, acc_ref, xb_ref, interleave=None):
    xb_ref[rows, :] = x_ref[rows, :].astype(BF16)
    lo = 0
    for width in FF_CHUNKS:
        if interleave is not None:
            next(interleave, None)
        xb = xb_ref[rows, :]
        gate = jnp.dot(xb, wi_ref[:, lo:lo + width], preferred_element_type=F32)
        up = jnp.dot(xb, wi_ref[:, D_FF + lo:D_FF + lo + width], preferred_element_type=F32)
        half = 0.5 * gate
        act = ((half * jnp.tanh(half) + half) * up).astype(BF16)
        part = jnp.dot(act, wo_ref[lo:lo + width, :], preferred_element_type=F32)
        if lo == 0:
            acc_ref[rows, :] = part
        else:
            acc_ref[rows, :] += part
        lo += width
    y = ALPHA * x_ref[rows, :] + 0.5 * acc_ref[rows, :]
    o_ref[rows, :] = _layernorm(y, g_ref[...], b_ref[...])


def _cast_slab(src_refs, dst_refs):
    for src, dst in zip(src_refs, dst_refs):
        dst[...] = src[...].astype(BF16)


def _ffn_kernel(*refs, cast_next, proj):
    if proj:
        res_ref, a_ref, wp_ref, gp_ref, bp_ref = refs[:5]
        refs = refs[5:]
    else:
        x_ref, refs = refs[0], refs[1:]
    wi_ref, wo_ref, g_ref, b_ref = refs[:4]
    if cast_next:
        nwi_ref, nwo_ref, o_ref, cwi_ref, cwo_ref, acc_ref, xb_ref = refs[4:]
        _cast_slab((nwi_ref, nwo_ref), (cwi_ref, cwo_ref))
    else:
        o_ref, acc_ref, xb_ref = refs[4:]
    def build_input(rows):
        o_ref[rows, :] = ALPHA * res_ref[rows, :] + jnp.dot(a_ref[rows, :], wp_ref[...],
                                                            preferred_element_type=F32)
        yield
        o_ref[rows, :] = _layernorm(o_ref[rows, :], gp_ref[...], bp_ref[...])
        yield

    n_sub = TM_FFN // SUB_FFN
    sub_rows = [slice(s * SUB_FFN, (s + 1) * SUB_FFN) for s in range(n_sub)]
    if proj:
        x_ref = o_ref
        for _ in build_input(sub_rows[0]):
            pass
    for s in range(n_sub):
        ahead = build_input(sub_rows[s + 1]) if proj and s + 1 < n_sub else None
        _ffn_rows(x_ref, sub_rows[s], wi_ref, wo_ref, g_ref, b_ref, o_ref, acc_ref, xb_ref,
                  interleave=ahead)


def _cast_specs(layer, slabs):
    ri, ro = D_MODEL // slabs, D_FF // slabs
    assert ri * slabs == D_MODEL and ro * slabs == D_FF and ri % 16 == 0 and ro % 16 == 0
    slab = lambda i: jnp.minimum(i, slabs - 1)
    in_specs = [pl.BlockSpec((None, ri, 2 * D_FF), lambda i: (layer, slab(i), 0)),
                pl.BlockSpec((None, ro, D_MODEL), lambda i: (layer, slab(i), 0))]
    out_specs = [pl.BlockSpec((ri, 2 * D_FF), lambda i: (slab(i), 0)),
                 pl.BlockSpec((ro, D_MODEL), lambda i: (slab(i), 0))]
    out_shape = [jax.ShapeDtypeStruct((D_MODEL, 2 * D_FF), BF16),
                 jax.ShapeDtypeStruct((D_FF, D_MODEL), BF16)]
    return in_specs, out_specs, out_shape


def _ffn_ln(x, wi, wo, ln_g, ln_b, ln_idx, next_w=None, proj=None):
    t = x.shape[0]
    steps = t // TM_FFN
    tile = pl.BlockSpec((TM_FFN, D_MODEL), lambda i: (i, 0))
    in_specs, lead = [tile], [x]
    if proj is not None:
        a, wp, player, pidx = proj
        in_specs += [pl.BlockSpec((TM_FFN, a.shape[1]), lambda i: (i, 0)),
                     _resident(wp.shape[1:], player),
                     _resident((1, D_MODEL), pidx), _resident((1, D_MODEL), pidx)]
        lead += [a, wp, ln_g, ln_b]
    in_specs += [_resident((D_MODEL, 2 * D_FF)), _resident((D_FF, D_MODEL)),
                 _resident((1, D_MODEL), ln_idx), _resident((1, D_MODEL), ln_idx)]
    out_specs, out_shape, args = [tile], [jax.ShapeDtypeStruct((t, D_MODEL), F32)], []
    if next_w is not None:
        cast_in, cast_out, cast_shape = _cast_specs(next_w[2], steps)
        in_specs += cast_in
        out_specs += cast_out
        out_shape += cast_shape
        args = list(next_w[:2])
    return pl.pallas_call(
        functools.partial(_ffn_kernel, cast_next=next_w is not None, proj=proj is not None),
        grid=(steps,),
        in_specs=in_specs,
        out_specs=out_specs,
        out_shape=out_shape,
        scratch_shapes=[pltpu.VMEM((TM_FFN, D_MODEL), F32), pltpu.VMEM((TM_FFN, D_MODEL), BF16)],
        compiler_params=_params(("parallel",), 58),
        name="ffn_ln",
    )(*lead, wi, wo, ln_g, ln_b, *args)


def _qkv_kernel(x_ref, w_ref, cos_ref, sa_ref, sb_ref, q_ref, k_ref, v_ref):
    z = jnp.dot(x_ref[...].astype(BF16), w_ref[...], preferred_element_type=F32)
    cos, sa, sb = cos_ref[...], sa_ref[...], sb_ref[...]

    def rope(t):
        return t * cos + pltpu.roll(t, LANES - 32, 1) * sa + pltpu.roll(t, 32, 1) * sb

    nq = N_HEADS * HEAD_DIM
    nk = N_KV_HEADS * HEAD_DIM
    for c in range(nq // LANES):
        sl = slice(c * LANES, (c + 1) * LANES)
        q_ref[:, sl] = (rope(z[:, sl]) * (HEAD_DIM ** -0.5)).astype(BF16)
    for c in range(nk // LANES):
        sl = slice(c * LANES, (c + 1) * LANES)
        k_ref[:, sl] = rope(z[:, nq + c * LANES:nq + (c + 1) * LANES]).astype(BF16)
    v_ref[...] = z[:, nq + nk:].astype(BF16)


def _qkv_rope(x, w, cos, sa, sb, seq, layer):
    t = x.shape[0]
    nq = N_HEADS * HEAD_DIM
    nk = N_KV_HEADS * HEAD_DIM
    tiles_per_seq = seq // TM_PROJ
    tab = pl.BlockSpec((TM_PROJ, LANES), lambda i: (i % tiles_per_seq, 0))
    row = lambda n: pl.BlockSpec((TM_PROJ, n), lambda i: (i, 0))
    return pl.pallas_call(
        _qkv_kernel,
        grid=(t // TM_PROJ,),
        in_specs=[row(D_MODEL), _resident((D_MODEL, QKV_COLS), layer), tab, tab, tab],
        out_specs=[row(nq), row(nk), row(nk)],
        out_shape=[jax.ShapeDtypeStruct((t, nq), BF16), jax.ShapeDtypeStruct((t, nk), BF16),
                   jax.ShapeDtypeStruct((t, nk), BF16)],
        compiler_params=_params(("parallel",), 32),
        name="qkv_rope",
    )(x, w, cos, sa, sb)


def _attn_kernel(sink_ref, q_ref, kc_ref, kp_ref, vc_ref, vp_ref, o_ref, *, tiles_per_seq):
    nblk = TQ_ATTN // WINDOW
    first = (pl.program_id(0) % tiles_per_seq) == 0
    kfull = jnp.concatenate([kp_ref[...], kc_ref[...]], axis=0)
    vfull = jnp.concatenate([vp_ref[...], vc_ref[...]], axis=0)

    qi = lax.broadcasted_iota(jnp.int32, (WINDOW, 2 * WINDOW), 0)
    kj = lax.broadcasted_iota(jnp.int32, (WINDOW, 2 * WINDOW), 1)
    band = (kj > qi) & (kj <= qi + WINDOW)
    band0 = band & (kj >= jnp.where(first, WINDOW, 0))
    col0 = kj[0:1, :] == 0
    lane_k = lax.broadcasted_iota(jnp.int32, (2 * WINDOW, LANES), 1)
    key0 = lax.broadcasted_iota(jnp.int32, (2 * WINDOW, LANES), 0) == 0
    lane_o = lax.broadcasted_iota(jnp.int32, (WINDOW, LANES), 1)
    ones = jnp.ones((2 * WINDOW, LANES), BF16)
    neg = jnp.finfo(F32).min

    for j in range(nblk):
        mask = band0 if j == 0 else band
        rows = slice(j * WINDOW, (j + 1) * WINDOW)
        keys = slice(j * WINDOW, (j + 2) * WINDOW)
        for kp in range(N_KV_HEADS // 2):
            pair = slice(kp * LANES, (kp + 1) * LANES)
            kblk = kfull[keys, pair]
            k_halves = (jnp.where(lane_k < HEAD_DIM, kblk, jnp.zeros_like(kblk)),
                        jnp.where(lane_k >= HEAD_DIM, kblk, jnp.zeros_like(kblk)))
            qstack = jnp.concatenate(
                [q_ref[rows, (kp * GROUP + c) * LANES:(kp * GROUP + c + 1) * LANES]
                 for c in range(GROUP)], axis=0)
            probs = []
            for half in range(2):
                s = lax.dot_general(qstack, k_halves[half], (((1,), (1,)), ((), ())),
                                    preferred_element_type=F32)
                for c in range(GROUP):
                    sink = sink_ref[(2 * kp + half) * GROUP + c]
                    fill = jnp.where(col0, sink, neg)
                    sc = jnp.where(mask, s[c * WINDOW:(c + 1) * WINDOW, :], fill)
                    m = jnp.max(sc, axis=-1, keepdims=True)
                    probs.append(jnp.exp(sc - m).astype(BF16))
            p = jnp.concatenate(probs, axis=0)
            vblk = vfull[keys, pair]
            vext = jnp.concatenate([jnp.where(key0, jnp.zeros_like(vblk), vblk), ones], axis=1)
            pv = jnp.dot(p, vext, preferred_element_type=F32)
            for c in range(GROUP):
                nums, dens = [], []
                for half in range(2):
                    r = (half * GROUP + c) * WINDOW
                    nums.append(pv[r:r + WINDOW, :LANES])
                    dens.append(pv[r:r + WINDOW, LANES:])
                low = lane_o < HEAD_DIM
                chunk = jnp.where(low, nums[0], nums[1]) / jnp.where(low, dens[0], dens[1])
                col = (kp * GROUP + c) * LANES
                o_ref[rows, col:col + LANES] = chunk.astype(BF16)


def _swa_attn(sinks, q, k, v, seq):
    t = q.shape[0]
    nq = N_HEADS * HEAD_DIM
    nk = N_KV_HEADS * HEAD_DIM
    blk_per_tile = TQ_ATTN // WINDOW
    cur = lambda n: pl.BlockSpec((TQ_ATTN, n), lambda i: (i, 0))
    prev = pl.BlockSpec((WINDOW, nk), lambda i: (jnp.maximum(i * blk_per_tile - 1, 0), 0))
    return pl.pallas_call(
        functools.partial(_attn_kernel, tiles_per_seq=seq // TQ_ATTN),
        grid=(t // TQ_ATTN,),
        in_specs=[pl.BlockSpec(memory_space=pltpu.SMEM), cur(nq), cur(nk), prev, cur(nk), prev],
        out_specs=cur(nq),
        out_shape=jax.ShapeDtypeStruct((t, nq), BF16),
        compiler_params=_params(("parallel",), 32),
        name="swa_attn",
    )(sinks, q, k, k, v, v)


def _lru_pieces(x_ref, win_ref, cw_ref, cb_ref, wra_ref, bra_ref, wrx_ref, brx_ref, lam_ref,
                wout_ref, g_ref, b_ref, out_ref, z_ref, hb_ref, xpad_ref, carry_ref):
    tm = TM_LRU
    tail = (CONV_W - 1) * SUBLANES
    z = jnp.dot(x_ref[...].astype(BF16), win_ref[...], preferred_element_type=F32)
    for c in range(2 * N_SLAB):
        for s in range(SUBLANES):
            z_ref[c, pl.ds(s, SEG, stride=SUBLANES), :] = (
                z[s * SEG:(s + 1) * SEG, c * LANES:(c + 1) * LANES])
    yield
    for c in range(N_SLAB, 2 * N_SLAB):
        z_ref[c] = jax.nn.gelu(z_ref[c])
    yield

    for part in range(2):
        shape3 = (CONV_W - 1, SUBLANES, LANES)
        sub = lax.broadcasted_iota(jnp.int32, shape3, 1)
        for c in range(part * N_SLAB // 2, (part + 1) * N_SLAB // 2):
            lanes = slice(c * LANES, (c + 1) * LANES)
            cw = cw_ref[:, lanes]
            xb = z_ref[c]
            last = xb[tm - tail:, :]
            head = jnp.where(sub == 0, pltpu.roll(xpad_ref[c].reshape(shape3), 1, 1),
                             pltpu.roll(last.reshape(shape3), 1, 1)).reshape(tail, LANES)
            ext = jnp.concatenate([head, xb], axis=0)
            xc = cb_ref[:, lanes] + cw[CONV_W - 1:CONV_W, :] * xb
            for d in range(1, CONV_W):
                lo = tail - d * SUBLANES
                xc = xc + cw[CONV_W - 1 - d:CONV_W - d, :] * ext[lo:lo + tm, :]
            hb_ref[c] = xc
            xpad_ref[c] = last
        yield

    slabs_per_block = RNN_BLOCK_W // LANES
    for n in range(RNN_BLOCKS):
        sl = slice(n * RNN_BLOCK_W, (n + 1) * RNN_BLOCK_W)
        slabs = range(n * slabs_per_block, (n + 1) * slabs_per_block)
        lam = lam_ref[:, sl]
        c_log_sig = LRU_C * (jnp.minimum(lam, 0.0) - jnp.log1p(jnp.exp(-jnp.abs(lam))))
        xn = jnp.concatenate([hb_ref[c] for c in slabs], axis=1)
        xr = xn.astype(BF16)
        r = _sigmoid(jnp.dot(xr, wra_ref[n], preferred_element_type=F32) + bra_ref[:, sl])
        i = _sigmoid(jnp.dot(xr, wrx_ref[n], preferred_element_type=F32) + brx_ref[:, sl])
        log_a = r * c_log_sig
        a = jnp.exp(log_a)
        q = -jnp.tanh(log_a) * (a * a + 1.0)
        root = jnp.where(q == 0.0, 0.0, q * lax.rsqrt(q))
        b = root * (i * xn)
        for j, c in enumerate(slabs):
            z_ref[c] = a[:, j * LANES:(j + 1) * LANES]
            hb_ref[c] = b[:, j * LANES:(j + 1) * LANES]
        if n % 2 == 1:
            yield

    steps_per_piece = SEG // LRU_SCAN_PIECES
    state = [(jnp.zeros((SUBLANES, LANES), F32), jnp.ones((SUBLANES, LANES), F32))] * N_SLAB
    for piece in range(LRU_SCAN_PIECES):
        for k in range(piece * steps_per_piece, (piece + 1) * steps_per_piece):
            rows = slice(k * SUBLANES, (k + 1) * SUBLANES)
            for c in range(N_SLAB):
                h, prod = state[c]
                a = z_ref[c, rows, :]
                h = a * h + hb_ref[c, rows, :]
                prod = a * prod
                hb_ref[c, rows, :] = h
                z_ref[c, rows, :] = prod
                state[c] = (h, prod)
        yield

    row = lax.broadcasted_iota(jnp.int32, (SUBLANES, LANES), 0)
    ys = []
    for c in range(N_SLAB):
        lanes = slice(c * LANES, (c + 1) * LANES)
        b, a = state[c]
        for d in (1, 2, 4):
            a_sh = jnp.where(row >= d, pltpu.roll(a, d, 0), 1.0)
            b_sh = jnp.where(row >= d, pltpu.roll(b, d, 0), 0.0)
            b = a * b_sh + b
            a = a * a_sh
        h_prev = carry_ref[:, lanes]
        h_end = a * h_prev + b
        h_in = jnp.where(row == 0, h_prev, pltpu.roll(h_end, 1, 0))
        carry_ref[:, lanes] = h_end[SUBLANES - 1:SUBLANES, :]
        h_in_all = jnp.broadcast_to(h_in[None], (SEG, SUBLANES, LANES)).reshape(tm, LANES)
        h = hb_ref[c] + z_ref[c] * h_in_all
        ys.append((h * z_ref[N_SLAB + c]).astype(BF16))
    proj = jnp.dot(jnp.concatenate(ys, axis=1), wout_ref[...], preferred_element_type=F32)
    for c in range(N_SLAB):
        hb_ref[c] = proj[:, c * LANES:(c + 1) * LANES]
    yield
    back = jnp.concatenate(
        [jnp.concatenate([hb_ref[c, pl.ds(s, SEG, stride=SUBLANES), :] for s in range(SUBLANES)],
                         axis=0) for c in range(N_SLAB)], axis=1)
    out_ref[...] = _layernorm(ALPHA * x_ref[...] + back, g_ref[...], b_ref[...])


N_SLAB = D_RNN // LANES
SEG = TM_LRU // SUBLANES
LRU_SCAN_PIECES = 4

N_LRU_REFS = 12


def _lru_ffn_kernel(*refs, tiles_per_seq, cast_next):
    lru_in, refs = refs[:N_LRU_REFS], refs[N_LRU_REFS:]
    wi_ref, wo_ref, g2_ref, b2_ref = refs[:4]
    if cast_next:
        nwi_ref, nwo_ref, o_ref, cwi_ref, cwo_ref = refs[4:9]
        scratch = refs[9:]
        _cast_slab((nwi_ref, nwo_ref), (cwi_ref, cwo_ref))
    else:
        o_ref = refs[4]
        scratch = refs[5:]
    mid_ref, xin_ref, acc_ref, xb_ref, z_ref, hb_ref, xpad_ref, carry_ref = scratch
    step = pl.program_id(0)

    @pl.when(step == 0)
    def _():
        mid_ref[...] = jnp.zeros((TM_LRU, D_MODEL), F32)

    @pl.when(step % tiles_per_seq == 0)
    def _():
        xpad_ref[...] = jnp.zeros(xpad_ref.shape, F32)
        carry_ref[...] = jnp.zeros((1, D_RNN), F32)

    xin_ref[...] = mid_ref[...]
    pieces = _lru_pieces(*lru_in, mid_ref, z_ref, hb_ref, xpad_ref, carry_ref)
    _ffn_rows(xin_ref, slice(0, TM_LRU), wi_ref, wo_ref, g2_ref, b2_ref, o_ref, acc_ref, xb_ref,
              interleave=pieces)
    for _ in pieces:
        pass


def _lru_ffn(x, lru_w, ln_g, ln_b, wi, wo, seq, layer, ln_lru, ln_ffn, next_w=None):
    t = x.shape[0]
    tiles = t // TM_LRU
    tile_in = pl.BlockSpec((TM_LRU, D_MODEL), lambda i: (jnp.minimum(i, tiles - 1), 0))
    tile_out = pl.BlockSpec((TM_LRU, D_MODEL), lambda i: (jnp.maximum(i - 1, 0), 0))
    vec = _resident((1, D_RNN), layer)
    gates = _resident((RNN_BLOCKS, RNN_BLOCK_W, RNN_BLOCK_W), layer)
    in_specs = [tile_in, _resident((D_MODEL, 2 * D_RNN), layer), _resident((CONV_W, D_RNN), layer),
                vec, gates, vec, gates, vec, vec, _resident((D_RNN, D_MODEL), layer),
                _resident((1, D_MODEL), ln_lru), _resident((1, D_MODEL), ln_lru),
                _resident((D_MODEL, 2 * D_FF)), _resident((D_FF, D_MODEL)),
                _resident((1, D_MODEL), ln_ffn), _resident((1, D_MODEL), ln_ffn)]
    assert len(in_specs) == N_LRU_REFS + 4
    out_specs, out_shape, args = [tile_out], [jax.ShapeDtypeStruct((t, D_MODEL), F32)], []
    if next_w is not None:
        cast_in, cast_out, cast_shape = _cast_specs(next_w[2], 16)
        in_specs += cast_in
        out_specs += cast_out
        out_shape += cast_shape
        args = list(next_w[:2])
    return pl.pallas_call(
        functools.partial(_lru_ffn_kernel, tiles_per_seq=seq // TM_LRU,
                          cast_next=next_w is not None),
        grid=(tiles + 1,),
        in_specs=in_specs,
        out_specs=out_specs,
        out_shape=out_shape,
        scratch_shapes=[pltpu.VMEM((TM_LRU, D_MODEL), F32),
                        pltpu.VMEM((TM_LRU, D_MODEL), F32),
                        pltpu.VMEM((TM_LRU, D_MODEL), F32),
                        pltpu.VMEM((TM_LRU, D_MODEL), BF16),
                        pltpu.VMEM((2 * N_SLAB, TM_LRU, LANES), F32),
                        pltpu.VMEM((N_SLAB, TM_LRU, LANES), F32),
                        pltpu.VMEM((N_SLAB, (CONV_W - 1) * SUBLANES, LANES), F32),
                        pltpu.VMEM((1, D_RNN), F32)],
        compiler_params=_params(("arbitrary",), 58),
        name="lru_ffn",
    )(x, *lru_w, ln_g, ln_b, wi, wo, ln_g, ln_b, *args)


def _rope_tables(seq):
    inv_freq = ROPE_THETA ** (-jnp.arange(0, HEAD_DIM, 2, dtype=F32) / HEAD_DIM)
    hi = jnp.arange(0, seq, ROPE_SPLIT, dtype=F32)[:, None] * inv_freq[None, :]
    lo = jnp.arange(ROPE_SPLIT, dtype=F32)[:, None] * inv_freq[None, :]
    ch, sh = jnp.cos(hi)[:, None, :], jnp.sin(hi)[:, None, :]
    cl, sl = jnp.cos(lo)[None, :, :], jnp.sin(lo)[None, :, :]
    cos = (ch * cl - sh * sl).reshape(seq, HEAD_DIM // 2)
    sin = (sh * cl + ch * sl).reshape(seq, HEAD_DIM // 2)
    zero = jnp.zeros_like(sin)
    cos_t = jnp.concatenate([cos, cos, cos, cos], axis=-1)
    sin_a = jnp.concatenate([-sin, zero, -sin, zero], axis=-1)
    sin_b = jnp.concatenate([zero, sin, zero, sin], axis=-1)
    return cos_t, sin_a, sin_b


def _permute_q_heads(w_qkv, w_o):
    nq = N_HEADS * HEAD_DIM
    n = w_qkv.shape[0]
    perm = jnp.asarray(Q_HEAD_PERM)
    wq = w_qkv[:, :, :nq].reshape(n, D_MODEL, N_HEADS, HEAD_DIM)[:, :, perm, :]
    w_qkv_p = jnp.concatenate([wq.reshape(n, D_MODEL, nq), w_qkv[:, :, nq:]], axis=2)
    w_o_p = w_o.reshape(n, N_HEADS, HEAD_DIM, D_MODEL)[:, perm].reshape(n, nq, D_MODEL)
    return w_qkv_p, w_o_p


def kernel(x, ffn1_w_in, ffn1_w_out, ffn2_w_in, ffn2_w_out, ln_g, ln_b, attn_w_qkv, attn_sinks,
           attn_w_o, lru_w_in, lru_conv_w, lru_conv_b, lru_w_ra, lru_b_ra, lru_w_rx, lru_b_rx,
           lru_lambda, lru_w_out):
    batch, seq, _ = x.shape
    cos_t, sin_a, sin_b = _rope_tables(seq)
    bf = lambda a: a.astype(BF16)
    vecs = lambda a: a.reshape(-1, 1, a.shape[-1])
    w_qkv, w_o = (bf(w) for w in _permute_q_heads(attn_w_qkv, attn_w_o))
    lru = (bf(lru_w_in), lru_conv_w, vecs(lru_conv_b), bf(lru_w_ra), vecs(lru_b_ra),
           bf(lru_w_rx), vecs(lru_b_rx), vecs(lru_lambda), bf(lru_w_out))
    g, b = vecs(ln_g), vecs(ln_b)

    h = x.reshape(batch * seq, D_MODEL)
    wi, wo = bf(ffn1_w_in[0]), bf(ffn1_w_out[0])
    for i in range(DEPTH):
        h, wi, wo = _ffn_ln(h, wi, wo, g, b, 3 * i, next_w=(ffn2_w_in, ffn2_w_out, i))
        j = i // 2
        next_w = (ffn1_w_in, ffn1_w_out, i + 1) if i + 1 < DEPTH else None
        if i % 2 == 0:
            q, k, v = _qkv_rope(h, w_qkv, cos_t, sin_a, sin_b, seq, j)
            o = _swa_attn(attn_sinks[j], q, k, v, seq)
            outs = _ffn_ln(h, wi, wo, g, b, 3 * i + 2, next_w=next_w, proj=(o, w_o, j, 3 * i + 1))
        else:
            outs = _lru_ffn(h, lru, g, b, wi, wo, seq, j, 3 * i + 1, 3 * i + 2, next_w=next_w)
        h = outs[0]
        if next_w is not None:
            wi, wo = outs[1:]
    return h.reshape(batch, seq, D_MODEL)
```

```python
import functools

import jax
import jax.numpy as jnp
from jax import lax
from jax.experimental import pallas as pl
from jax.experimental.pallas import tpu as pltpu

F32 = jnp.float32
BF16 = jnp.bfloat16

D_MODEL = 1024
DEPTH = 4
N_HEADS = 16
N_KV_HEADS = 4
HEAD_DIM = 64
GROUP = N_HEADS // N_KV_HEADS
WINDOW = 128
ROPE_THETA = 10000.0
D_RNN = 1024
RNN_BLOCKS = 4
RNN_BLOCK_W = D_RNN // RNN_BLOCKS
CONV_W = 4
LRU_C = 8.0
D_FF = 2816
ALPHA = (2.0 * DEPTH) ** 0.25
LN_EPS = 1e-5
QKV_COLS = (N_HEADS + 2 * N_KV_HEADS) * HEAD_DIM

LANES = 128
SUBLANES = 8
MIB = 1024 * 1024

TM_FFN = 1024
SUB_FFN = 512
FF_CHUNKS = (256,) * 11
assert sum(FF_CHUNKS) == D_FF
TM_PROJ = 1024
ROPE_SPLIT = 64
TQ_ATTN = 512
TM_LRU = 512


def _resident(shape, layer=None):
    nd = len(shape)
    if layer is None:
        return pl.BlockSpec(tuple(shape), lambda *_: (0,) * nd, pipeline_mode=pl.Buffered(1))
    return pl.BlockSpec((None,) + tuple(shape), lambda *_: (layer,) + (0,) * nd,
                        pipeline_mode=pl.Buffered(1))


def _sigmoid(x):
    return 0.5 * jnp.tanh(0.5 * x) + 0.5


def _params(semantics, vmem_mib):
    return pltpu.CompilerParams(dimension_semantics=semantics, vmem_limit_bytes=vmem_mib * MIB)


def _layernorm(y, g, b):
    mu = jnp.mean(y, axis=-1, keepdims=True)
    yc = y - mu
    var = jnp.mean(yc * yc, axis=-1, keepdims=True)
    return yc * lax.rsqrt(var + LN_EPS) * g + b


def _ffn_rows(x_ref, rows, wi_ref, wo_ref, g_ref, b_ref, o_ref, acc_ref, xb_ref, interleave=None):
    xb_ref[rows, :] = x_ref[rows, :].astype(BF16)
    lo = 0
    for width in FF_CHUNKS:
        if interleave is not None:
            next(interleave, None)
        xb = xb_ref[rows, :]
        gate = jnp.dot(xb, wi_ref[:, lo:lo + width], preferred_element_type=F32)
        up = jnp.dot(xb, wi_ref[:, D_FF + lo:D_FF + lo + width], preferred_element_type=F32)
        half = 0.5 * gate
        act = ((half * jnp.tanh(half) + half) * up).astype(BF16)
        part = jnp.dot(act, wo_ref[lo:lo + width, :], preferred_element_type=F32)
        if lo == 0:
            acc_ref[rows, :] = part
        else:
            acc_ref[rows, :] += part
        lo += width
    y = ALPHA * x_ref[rows, :] + 0.5 * acc_ref[rows, :]
    o_ref[rows, :] = _layernorm(y, g_ref[...], b_ref[...])


def _cast_slab(src_refs, dst_refs):
    for src, dst in zip(src_refs, dst_refs):
        dst[...] = src[...].astype(BF16)


def _ffn_kernel(*refs, cast_next, proj):
    if proj:
        res_ref, a_ref, wp_ref, gp_ref, bp_ref = refs[:5]
        refs = refs[5:]
    else:
        x_ref, refs = refs[0], refs[1:]
    wi_ref, wo_ref, g_ref, b_ref = refs[:4]
    if cast_next:
        nwi_ref, nwo_ref, o_ref, cwi_ref, cwo_ref, acc_ref, xb_ref = refs[4:]
        _cast_slab((nwi_ref, nwo_ref), (cwi_ref, cwo_ref))
    else:
        o_ref, acc_ref, xb_ref = refs[4:]
    def build_input(rows):
        o_ref[rows, :] = ALPHA * res_ref[rows, :] + jnp.dot(a_ref[rows, :], wp_ref[...],
                                                            preferred_element_type=F32)
        yield
        o_ref[rows, :] = _layernorm(o_ref[rows, :], gp_ref[...], bp_ref[...])
        yield

    n_sub = TM_FFN // SUB_FFN
    sub_rows = [slice(s * SUB_FFN, (s + 1) * SUB_FFN) for s in range(n_sub)]
    if proj:
        x_ref = o_ref
        for _ in build_input(sub_rows[0]):
            pass
    for s in range(n_sub):
        ahead = build_input(sub_rows[s + 1]) if proj and s + 1 < n_sub else None
        _ffn_rows(x_ref, sub_rows[s], wi_ref, wo_ref, g_ref, b_ref, o_ref, acc_ref, xb_ref,
                  interleave=ahead)


def _cast_specs(layer, slabs):
    ri, ro = D_MODEL // slabs, D_FF // slabs
    assert ri * slabs == D_MODEL and ro * slabs == D_FF and ri % 16 == 0 and ro % 16 == 0
    slab = lambda i: jnp.minimum(i, slabs - 1)
    in_specs = [pl.BlockSpec((None, ri, 2 * D_FF), lambda i: (layer, slab(i), 0)),
                pl.BlockSpec((None, ro, D_MODEL), lambda i: (layer, slab(i), 0))]
    out_specs = [pl.BlockSpec((ri, 2 * D_FF), lambda i: (slab(i), 0)),
                 pl.BlockSpec((ro, D_MODEL), lambda i: (slab(i), 0))]
    out_shape = [jax.ShapeDtypeStruct((D_MODEL, 2 * D_FF), BF16),
                 jax.ShapeDtypeStruct((D_FF, D_MODEL), BF16)]
    return in_specs, out_specs, out_shape


def _ffn_ln(x, wi, wo, ln_g, ln_b, ln_idx, next_w=None, proj=None):
    t = x.shape[0]
    steps = t // TM_FFN
    tile = pl.BlockSpec((TM_FFN, D_MODEL), lambda i: (i, 0))
    in_specs, lead = [tile], [x]
    if proj is not None:
        a, wp, player, pidx = proj
        in_specs += [pl.BlockSpec((TM_FFN, a.shape[1]), lambda i: (i, 0)),
                     _resident(wp.shape[1:], player),
                     _resident((1, D_MODEL), pidx), _resident((1, D_MODEL), pidx)]
        lead += [a, wp, ln_g, ln_b]
    in_specs += [_resident((D_MODEL, 2 * D_FF)), _resident((D_FF, D_MODEL)),
                 _resident((1, D_MODEL), ln_idx), _resident((1, D_MODEL), ln_idx)]
    out_specs, out_shape, args = [tile], [jax.ShapeDtypeStruct((t, D_MODEL), F32)], []
    if next_w is not None:
        cast_in, cast_out, cast_shape = _cast_specs(next_w[2], steps)
        in_specs += cast_in
        out_specs += cast_out
        out_shape += cast_shape
        args = list(next_w[:2])
    return pl.pallas_call(
        functools.partial(_ffn_kernel, cast_next=next_w is not None, proj=proj is not None),
        grid=(steps,),
        in_specs=in_specs,
        out_specs=out_specs,
        out_shape=out_shape,
        scratch_shapes=[pltpu.VMEM((TM_FFN, D_MODEL), F32), pltpu.VMEM((TM_FFN, D_MODEL), BF16)],
        compiler_params=_params(("parallel",), 58),
        name="ffn_ln",
    )(*lead, wi, wo, ln_g, ln_b, *args)


def _qkv_kernel(x_ref, w_ref, cos_ref, sa_ref, sb_ref, q_ref, k_ref, v_ref):
    z = jnp.dot(x_ref[...].astype(BF16), w_ref[...], preferred_element_type=F32)
    cos, sa, sb = cos_ref[...], sa_ref[...], sb_ref[...]

    def rope(t):
        return t * cos + pltpu.roll(t, LANES - 32, 1) * sa + pltpu.roll(t, 32, 1) * sb

    nq = N_HEADS * HEAD_DIM
    nk = N_KV_HEADS * HEAD_DIM
    for c in range(nq // LANES):
        sl = slice(c * LANES, (c + 1) * LANES)
        q_ref[:, sl] = (rope(z[:, sl]) * (HEAD_DIM ** -0.5)).astype(BF16)
    for c in range(nk // LANES):
        sl = slice(c * LANES, (c + 1) * LANES)
        k_ref[:, sl] = rope(z[:, nq + c * LANES:nq + (c + 1) * LANES]).astype(BF16)
    v_ref[...] = z[:, nq + nk:].astype(BF16)


def _qkv_rope(x, w, cos, sa, sb, seq, layer):
    t = x.shape[0]
    nq = N_HEADS * HEAD_DIM
    nk = N_KV_HEADS * HEAD_DIM
    tiles_per_seq = seq // TM_PROJ
    tab = pl.BlockSpec((TM_PROJ, LANES), lambda i: (i % tiles_per_seq, 0))
    row = lambda n: pl.BlockSpec((TM_PROJ, n), lambda i: (i, 0))
    return pl.pallas_call(
        _qkv_kernel,
        grid=(t // TM_PROJ,),
        in_specs=[row(D_MODEL), _resident((D_MODEL, QKV_COLS), layer), tab, tab, tab],
        out_specs=[row(nq), row(nk), row(nk)],
        out_shape=[jax.ShapeDtypeStruct((t, nq), BF16), jax.ShapeDtypeStruct((t, nk), BF16),
                   jax.ShapeDtypeStruct((t, nk), BF16)],
        compiler_params=_params(("parallel",), 32),
        name="qkv_rope",
    )(x, w, cos, sa, sb)


def _attn_kernel(sink_ref, q_ref, kc_ref, kp_ref, vc_ref, vp_ref, o_ref, *, tiles_per_seq):
    nblk = TQ_ATTN // WINDOW
    first = (pl.program_id(0) % tiles_per_seq) == 0
    kfull = jnp.concatenate([kp_ref[...], kc_ref[...]], axis=0)
    vfull = jnp.concatenate([vp_ref[...], vc_ref[...]], axis=0)

    qi = lax.broadcasted_iota(jnp.int32, (WINDOW, 2 * WINDOW), 0)
    kj = lax.broadcasted_iota(jnp.int32, (WINDOW, 2 * WINDOW), 1)
    band = (kj > qi) & (kj <= qi + WINDOW)
    band0 = band & (kj >= jnp.where(first, WINDOW, 0))
    col0 = kj[0:1, :] == 0
    lane_k = lax.broadcasted_iota(jnp.int32, (2 * WINDOW, LANES), 1)
    key0 = lax.broadcasted_iota(jnp.int32, (2 * WINDOW, LANES), 0) == 0
    lane_o = lax.broadcasted_iota(jnp.int32, (WINDOW, LANES), 1)
    ones = jnp.ones((2 * WINDOW, LANES), BF16)
    neg = jnp.finfo(F32).min

    for j in range(nblk):
        mask = band0 if j == 0 else band
        rows = slice(j * WINDOW, (j + 1) * WINDOW)
        keys = slice(j * WINDOW, (j + 2) * WINDOW)
        for kp in range(N_KV_HEADS // 2):
            pair = slice(kp * LANES, (kp + 1) * LANES)
            kblk = kfull[keys, pair]
            k_halves = (jnp.where(lane_k < HEAD_DIM, kblk, jnp.zeros_like(kblk)),
                        jnp.where(lane_k >= HEAD_DIM, kblk, jnp.zeros_like(kblk)))
            qstack = jnp.concatenate(
                [q_ref[rows, (kp * GROUP + c) * LANES:(kp * GROUP + c + 1) * LANES]
                 for c in range(GROUP)], axis=0)
            probs = []
            for half in range(2):
                s = lax.dot_general(qstack, k_halves[half], (((1,), (1,)), ((), ())),
                                    preferred_element_type=F32)
                for c in range(GROUP):
                    sink = sink_ref[(2 * kp + half) * GROUP + c]
                    fill = jnp.where(col0, sink, neg)
                    sc = jnp.where(mask, s[c * WINDOW:(c + 1) * WINDOW, :], fill)
                    m = jnp.max(sc, axis=-1, keepdims=True)
                    probs.append(jnp.exp(sc - m).astype(BF16))
            p = jnp.concatenate(probs, axis=0)
            vblk = vfull[keys, pair]
            vext = jnp.concatenate([jnp.where(key0, jnp.zeros_like(vblk), vblk), ones], axis=1)
            pv = jnp.dot(p, vext, preferred_element_type=F32)
            for c in range(GROUP):
                nums, dens = [], []
                for half in range(2):
                    r = (half * GROUP + c) * WINDOW
                    nums.append(pv[r:r + WINDOW, :LANES])
                    dens.append(pv[r:r + WINDOW, LANES:])
                low = lane_o < HEAD_DIM
                chunk = jnp.where(low, nums[0], nums[1]) / jnp.where(low, dens[0], dens[1])
                col = (kp * GROUP + c) * LANES
                o_ref[rows, col:col + LANES] = chunk.astype(BF16)


def _swa_attn(sinks, q, k, v, seq):
    t = q.shape[0]
    nq = N_HEADS * HEAD_DIM
    nk = N_KV_HEADS * HEAD_DIM
    blk_per_tile = TQ_ATTN // WINDOW
    cur = lambda n: pl.BlockSpec((TQ_ATTN, n), lambda i: (i, 0))
    prev = pl.BlockSpec((WINDOW, nk), lambda i: (jnp.maximum(i * blk_per_tile - 1, 0), 0))
    return pl.pallas_call(
        functools.partial(_attn_kernel, tiles_per_seq=seq // TQ_ATTN),
        grid=(t // TQ_ATTN,),
        in_specs=[pl.BlockSpec(memory_space=pltpu.SMEM), cur(nq), cur(nk), prev, cur(nk), prev],
        out_specs=cur(nq),
        out_shape=jax.ShapeDtypeStruct((t, nq), BF16),
        compiler_params=_params(("parallel",), 32),
        name="swa_attn",
    )(sinks, q, k, k, v, v)


def _lru_pieces(x_ref, win_ref, cw_ref, cb_ref, wra_ref, bra_ref, wrx_ref, brx_ref, lam_ref,
                wout_ref, g_ref, b_ref, out_ref, z_ref, hb_ref, xpad_ref, carry_ref):
    tm = TM_LRU
    half = D_RNN // 2
    z_ref[...] = jnp.dot(x_ref[...].astype(BF16), win_ref[...], preferred_element_type=F32)
    z_ref[:, D_RNN:] = jax.nn.gelu(z_ref[:, D_RNN:])
    yield

    for lanes in (slice(0, half), slice(half, D_RNN)):
        sub = lax.broadcasted_iota(jnp.int32, (1, SUBLANES, half), 1)
        cw = cw_ref[:, lanes]
        xb = z_ref[:, lanes]
        groups = jnp.concatenate([xpad_ref[:, lanes], xb], axis=0)
        groups = groups.reshape(tm // SUBLANES + 1, SUBLANES, half)
        xc = cb_ref[:, lanes] + cw[CONV_W - 1:CONV_W, :] * xb
        for d in range(1, CONV_W):
            rot = pltpu.roll(groups, d, 1)
            shifted = jnp.where(sub >= d, rot[1:], rot[:-1]).reshape(tm, half)
            xc = xc + cw[CONV_W - 1 - d:CONV_W - d, :] * shifted
        hb_ref[:, lanes] = xc
        xpad_ref[:, lanes] = xb[tm - SUBLANES:, :]
        yield

    for n in range(RNN_BLOCKS):
        sl = slice(n * RNN_BLOCK_W, (n + 1) * RNN_BLOCK_W)
        lam = lam_ref[:, sl]
        c_log_sig = LRU_C * (jnp.minimum(lam, 0.0) - jnp.log1p(jnp.exp(-jnp.abs(lam))))
        xn = hb_ref[:, sl]
        xr = xn.astype(BF16)
        r = _sigmoid(jnp.dot(xr, wra_ref[n], preferred_element_type=F32) + bra_ref[:, sl])
        i = _sigmoid(jnp.dot(xr, wrx_ref[n], preferred_element_type=F32) + brx_ref[:, sl])
        log_a = r * c_log_sig
        a = jnp.exp(log_a)
        z_ref[:, sl] = a
        q = -jnp.tanh(log_a) * (a * a + 1.0)
        root = jnp.where(q == 0.0, 0.0, q * lax.rsqrt(q))
        hb_ref[:, sl] = root * (i * xn)
        if n % 2 == 1:
            yield

    groups_per_piece = tm // SUBLANES // LRU_SCAN_PIECES
    for piece in range(LRU_SCAN_PIECES):
        row = lax.broadcasted_iota(jnp.int32, (SUBLANES, D_RNN), 0)
        h_in = carry_ref[...]
        for gi in range(piece * groups_per_piece, (piece + 1) * groups_per_piece):
            rows = slice(gi * SUBLANES, (gi + 1) * SUBLANES)
            a = z_ref[rows, :D_RNN]
            b = hb_ref[rows, :]
            for d in (1, 2, 4):
                a_sh = jnp.where(row >= d, pltpu.roll(a, d, 0), 1.0)
                b_sh = jnp.where(row >= d, pltpu.roll(b, d, 0), 0.0)
                b = a * b_sh + b
                a = a * a_sh
            h = a * h_in + b
            hb_ref[rows, :] = h
            h_in = h[SUBLANES - 1:SUBLANES, :]
        carry_ref[...] = h_in
        yield

    y = (hb_ref[...] * z_ref[:, D_RNN:]).astype(BF16)
    z_ref[:, :D_RNN] = ALPHA * x_ref[...] + jnp.dot(y, wout_ref[...], preferred_element_type=F32)
    yield
    out_ref[...] = _layernorm(z_ref[:, :D_RNN], g_ref[...], b_ref[...])


LRU_SCAN_PIECES = 4


N_LRU_REFS = 12


def _lru_ffn_kernel(*refs, tiles_per_seq, cast_next):
    lru_in, refs = refs[:N_LRU_REFS], refs[N_LRU_REFS:]
    wi_ref, wo_ref, g2_ref, b2_ref = refs[:4]
    if cast_next:
        nwi_ref, nwo_ref, o_ref, cwi_ref, cwo_ref = refs[4:9]
        scratch = refs[9:]
        _cast_slab((nwi_ref, nwo_ref), (cwi_ref, cwo_ref))
    else:
        o_ref = refs[4]
        scratch = refs[5:]
    mid_ref, xin_ref, acc_ref, xb_ref, z_ref, hb_ref, xpad_ref, carry_ref = scratch
    step = pl.program_id(0)

    @pl.when(step == 0)
    def _():
        mid_ref[...] = jnp.zeros((TM_LRU, D_MODEL), F32)

    @pl.when(step % tiles_per_seq == 0)
    def _():
        xpad_ref[...] = jnp.zeros(xpad_ref.shape, F32)
        carry_ref[...] = jnp.zeros((1, D_RNN), F32)

    xin_ref[...] = mid_ref[...]
    pieces = _lru_pieces(*lru_in, mid_ref, z_ref, hb_ref, xpad_ref, carry_ref)
    _ffn_rows(xin_ref, slice(0, TM_LRU), wi_ref, wo_ref, g2_ref, b2_ref, o_ref, acc_ref, xb_ref,
              interleave=pieces)
    for _ in pieces:
        pass


def _lru_ffn(x, lru_w, ln_g, ln_b, wi, wo, seq, layer, ln_lru, ln_ffn, next_w=None):
    t = x.shape[0]
    tiles = t // TM_LRU
    tile_in = pl.BlockSpec((TM_LRU, D_MODEL), lambda i: (jnp.minimum(i, tiles - 1), 0))
    tile_out = pl.BlockSpec((TM_LRU, D_MODEL), lambda i: (jnp.maximum(i - 1, 0), 0))
    vec = _resident((1, D_RNN), layer)
    gates = _resident((RNN_BLOCKS, RNN_BLOCK_W, RNN_BLOCK_W), layer)
    in_specs = [tile_in, _resident((D_MODEL, 2 * D_RNN), layer), _resident((CONV_W, D_RNN), layer),
                vec, gates, vec, gates, vec, vec, _resident((D_RNN, D_MODEL), layer),
                _resident((1, D_MODEL), ln_lru), _resident((1, D_MODEL), ln_lru),
                _resident((D_MODEL, 2 * D_FF)), _resident((D_FF, D_MODEL)),
                _resident((1, D_MODEL), ln_ffn), _resident((1, D_MODEL), ln_ffn)]
    assert len(in_specs) == N_LRU_REFS + 4
    out_specs, out_shape, args = [tile_out], [jax.ShapeDtypeStruct((t, D_MODEL), F32)], []
    if next_w is not None:
        cast_in, cast_out, cast_shape = _cast_specs(next_w[2], 16)
        in_specs += cast_in
        out_specs += cast_out
        out_shape += cast_shape
        args = list(next_w[:2])
    return pl.pallas_call(
        functools.partial(_lru_ffn_kernel, tiles_per_seq=seq // TM_LRU,
                          cast_next=next_w is not None),
        grid=(tiles + 1,),
        in_specs=in_specs,
        out_specs=out_specs,
        out_shape=out_shape,
        scratch_shapes=[pltpu.VMEM((TM_LRU, D_MODEL), F32),
                        pltpu.VMEM((TM_LRU, D_MODEL), F32),
                        pltpu.VMEM((TM_LRU, D_MODEL), F32),
                        pltpu.VMEM((TM_LRU, D_MODEL), BF16),
                        pltpu.VMEM((TM_LRU, 2 * D_RNN), F32),
                        pltpu.VMEM((TM_LRU, D_RNN), F32),
                        pltpu.VMEM((SUBLANES, D_RNN), F32),
                        pltpu.VMEM((1, D_RNN), F32)],
        compiler_params=_params(("arbitrary",), 58),
        name="lru_ffn",
    )(x, *lru_w, ln_g, ln_b, wi, wo, ln_g, ln_b, *args)


def _rope_tables(seq):
    inv_freq = ROPE_THETA ** (-jnp.arange(0, HEAD_DIM, 2, dtype=F32) / HEAD_DIM)
    hi = jnp.arange(0, seq, ROPE_SPLIT, dtype=F32)[:, None] * inv_freq[None, :]
    lo = jnp.arange(ROPE_SPLIT, dtype=F32)[:, None] * inv_freq[None, :]
    ch, sh = jnp.cos(hi)[:, None, :], jnp.sin(hi)[:, None, :]
    cl, sl = jnp.cos(lo)[None, :, :], jnp.sin(lo)[None, :, :]
    cos = (ch * cl - sh * sl).reshape(seq, HEAD_DIM // 2)
    sin = (sh * cl + ch * sl).reshape(seq, HEAD_DIM // 2)
    zero = jnp.zeros_like(sin)
    cos_t = jnp.concatenate([cos, cos, cos, cos], axis=-1)
    sin_a = jnp.concatenate([-sin, zero, -sin, zero], axis=-1)
    sin_b = jnp.concatenate([zero, sin, zero, sin], axis=-1)
    return cos_t, sin_a, sin_b


def _permute_q_heads(w_qkv, w_o):
    nq = N_HEADS * HEAD_DIM
    n = w_qkv.shape[0]
    pairs = N_KV_HEADS // 2
    wq = w_qkv[:, :, :nq].reshape(n, D_MODEL, pairs, 2, GROUP, HEAD_DIM).swapaxes(3, 4)
    w_qkv_p = jnp.concatenate([wq.reshape(n, D_MODEL, nq), w_qkv[:, :, nq:]], axis=2)
    w_o_p = w_o.reshape(n, pairs, 2, GROUP, HEAD_DIM, D_MODEL).swapaxes(2, 3).reshape(n, nq, D_MODEL)
    return w_qkv_p, w_o_p


def kernel(x, ffn1_w_in, ffn1_w_out, ffn2_w_in, ffn2_w_out, ln_g, ln_b, attn_w_qkv, attn_sinks,
           attn_w_o, lru_w_in, lru_conv_w, lru_conv_b, lru_w_ra, lru_b_ra, lru_w_rx, lru_b_rx,
           lru_lambda, lru_w_out):
    batch, seq, _ = x.shape
    cos_t, sin_a, sin_b = _rope_tables(seq)
    bf = lambda a: a.astype(BF16)
    vecs = lambda a: a.reshape(-1, 1, a.shape[-1])
    w_qkv, w_o = (bf(w) for w in _permute_q_heads(attn_w_qkv, attn_w_o))
    lru = (bf(lru_w_in), lru_conv_w, vecs(lru_conv_b), bf(lru_w_ra), vecs(lru_b_ra),
           bf(lru_w_rx), vecs(lru_b_rx), vecs(lru_lambda), bf(lru_w_out))
    g, b = vecs(ln_g), vecs(ln_b)

    h = x.reshape(batch * seq, D_MODEL)
    wi, wo = bf(ffn1_w_in[0]), bf(ffn1_w_out[0])
    for i in range(DEPTH):
        h, wi, wo = _ffn_ln(h, wi, wo, g, b, 3 * i, next_w=(ffn2_w_in, ffn2_w_out, i))
        j = i // 2
        next_w = (ffn1_w_in, ffn1_w_out, i + 1) if i + 1 < DEPTH else None
        if i % 2 == 0:
            q, k, v = _qkv_rope(h, w_qkv, cos_t, sin_a, sin_b, seq, j)
            o = _swa_attn(attn_sinks[j], q, k, v, seq)
            outs = _ffn_ln(h, wi, wo, g, b, 3 * i + 2, next_w=next_w, proj=(o, w_o, j, 3 * i + 1))
        else:
            outs = _lru_ffn(h, lru, g, b, wi, wo, seq, j, 3 * i + 1, 3 * i + 2, next_w=next_w)
        h = outs[0]
        if next_w is not None:
            wi, wo = outs[1:]
    return h.reshape(batch, seq, D_MODEL)
```

```python
import functools

import jax
import jax.numpy as jnp
from jax import lax
from jax.experimental import pallas as pl
from jax.experimental.pallas import tpu as pltpu

F32 = jnp.float32
BF16 = jnp.bfloat16

D_MODEL = 1024
DEPTH = 4
N_HEADS = 16
N_KV_HEADS = 4
HEAD_DIM = 64
GROUP = N_HEADS // N_KV_HEADS
WINDOW = 128
ROPE_THETA = 10000.0
D_RNN = 1024
RNN_BLOCKS = 4
RNN_BLOCK_W = D_RNN // RNN_BLOCKS
CONV_W = 4
LRU_C = 8.0
D_FF = 2816
ALPHA = (2.0 * DEPTH) ** 0.25
LN_EPS = 1e-5
QKV_COLS = (N_HEADS + 2 * N_KV_HEADS) * HEAD_DIM

LANES = 128
SUBLANES = 8
MIB = 1024 * 1024

TM_FFN = 1024
SUB_FFN = 512
FF_CHUNKS = (256,) * 11
assert sum(FF_CHUNKS) == D_FF
TM_PROJ = 1024
ROPE_SPLIT = 64
TQ_ATTN = 512
TM_LRU = 512


def _resident(shape, layer=None):
    nd = len(shape)
    if layer is None:
        return pl.BlockSpec(tuple(shape), lambda *_: (0,) * nd, pipeline_mode=pl.Buffered(1))
    return pl.BlockSpec((None,) + tuple(shape), lambda *_: (layer,) + (0,) * nd,
                        pipeline_mode=pl.Buffered(1))


def _sigmoid(x):
    return 0.5 * jnp.tanh(0.5 * x) + 0.5


def _params(semantics, vmem_mib):
    return pltpu.CompilerParams(dimension_semantics=semantics, vmem_limit_bytes=vmem_mib * MIB)


def _layernorm(y, g, b):
    mu = jnp.mean(y, axis=-1, keepdims=True)
    yc = y - mu
    var = jnp.mean(yc * yc, axis=-1, keepdims=True)
    return yc * lax.rsqrt(var + LN_EPS) * g + b


def _ffn_rows(x_ref, rows, wi_ref, wo_ref, g_ref, b_ref, o_ref, act_ref, xb_ref, interleave=None):
    xb_ref[rows, :] = x_ref[rows, :].astype(BF16)
    lo = 0
    for width in FF_CHUNKS:
        if interleave is not None:
            next(interleave, None)
        xb = xb_ref[rows, :]
        gate = jnp.dot(xb, wi_ref[:, lo:lo + width], preferred_element_type=F32)
        up = jnp.dot(xb, wi_ref[:, D_FF + lo:D_FF + lo + width], preferred_element_type=F32)
        half = 0.5 * gate
        act_ref[rows, lo:lo + width] = ((half * jnp.tanh(half) + half) * up).astype(BF16)
        lo += width
    down = jnp.dot(act_ref[rows, :], wo_ref[...], preferred_element_type=F32)
    y = ALPHA * x_ref[rows, :] + 0.5 * down
    o_ref[rows, :] = _layernorm(y, g_ref[...], b_ref[...])


def _cast_slab(src_refs, dst_refs):
    for src, dst in zip(src_refs, dst_refs):
        dst[...] = src[...].astype(BF16)


def _ffn_kernel(*refs, cast_next, proj):
    if proj:
        res_ref, a_ref, wp_ref, gp_ref, bp_ref = refs[:5]
        refs = refs[5:]
    else:
        x_ref, refs = refs[0], refs[1:]
    wi_ref, wo_ref, g_ref, b_ref = refs[:4]
    if cast_next:
        nwi_ref, nwo_ref, o_ref, cwi_ref, cwo_ref, act_ref, xb_ref = refs[4:]
        _cast_slab((nwi_ref, nwo_ref), (cwi_ref, cwo_ref))
    else:
        o_ref, act_ref, xb_ref = refs[4:]
    def build_input(rows):
        o_ref[rows, :] = ALPHA * res_ref[rows, :] + jnp.dot(a_ref[rows, :], wp_ref[...],
                                                            preferred_element_type=F32)
        yield
        o_ref[rows, :] = _layernorm(o_ref[rows, :], gp_ref[...], bp_ref[...])
        yield

    n_sub = TM_FFN // SUB_FFN
    sub_rows = [slice(s * SUB_FFN, (s + 1) * SUB_FFN) for s in range(n_sub)]
    if proj:
        x_ref = o_ref
        for _ in build_input(sub_rows[0]):
            pass
    for s in range(n_sub):
        ahead = build_input(sub_rows[s + 1]) if proj and s + 1 < n_sub else None
        _ffn_rows(x_ref, sub_rows[s], wi_ref, wo_ref, g_ref, b_ref, o_ref, act_ref, xb_ref,
                  interleave=ahead)


def _cast_specs(layer, slabs):
    ri, ro = D_MODEL // slabs, D_FF // slabs
    assert ri * slabs == D_MODEL and ro * slabs == D_FF and ri % 16 == 0 and ro % 16 == 0
    slab = lambda i: jnp.minimum(i, slabs - 1)
    in_specs = [pl.BlockSpec((None, ri, 2 * D_FF), lambda i: (layer, slab(i), 0)),
                pl.BlockSpec((None, ro, D_MODEL), lambda i: (layer, slab(i), 0))]
    out_specs = [pl.BlockSpec((ri, 2 * D_FF), lambda i: (slab(i), 0)),
                 pl.BlockSpec((ro, D_MODEL), lambda i: (slab(i), 0))]
    out_shape = [jax.ShapeDtypeStruct((D_MODEL, 2 * D_FF), BF16),
                 jax.ShapeDtypeStruct((D_FF, D_MODEL), BF16)]
    return in_specs, out_specs, out_shape


def _ffn_ln(x, wi, wo, ln_g, ln_b, ln_idx, next_w=None, proj=None):
    t = x.shape[0]
    steps = t // TM_FFN
    tile = pl.BlockSpec((TM_FFN, D_MODEL), lambda i: (i, 0))
    in_specs, lead = [tile], [x]
    if proj is not None:
        a, wp, player, pidx = proj
        in_specs += [pl.BlockSpec((TM_FFN, a.shape[1]), lambda i: (i, 0)),
                     _resident(wp.shape[1:], player),
                     _resident((1, D_MODEL), pidx), _resident((1, D_MODEL), pidx)]
        lead += [a, wp, ln_g, ln_b]
    in_specs += [_resident((D_MODEL, 2 * D_FF)), _resident((D_FF, D_MODEL)),
                 _resident((1, D_MODEL), ln_idx), _resident((1, D_MODEL), ln_idx)]
    out_specs, out_shape, args = [tile], [jax.ShapeDtypeStruct((t, D_MODEL), F32)], []
    if next_w is not None:
        cast_in, cast_out, cast_shape = _cast_specs(next_w[2], steps)
        in_specs += cast_in
        out_specs += cast_out
        out_shape += cast_shape
        args = list(next_w[:2])
    return pl.pallas_call(
        functools.partial(_ffn_kernel, cast_next=next_w is not None, proj=proj is not None),
        grid=(steps,),
        in_specs=in_specs,
        out_specs=out_specs,
        out_shape=out_shape,
        scratch_shapes=[pltpu.VMEM((TM_FFN, D_FF), BF16), pltpu.VMEM((TM_FFN, D_MODEL), BF16)],
        compiler_params=_params(("parallel",), 58),
        name="ffn_ln",
    )(*lead, wi, wo, ln_g, ln_b, *args)


def _qkv_kernel(x_ref, w_ref, cos_ref, sa_ref, sb_ref, q_ref, k_ref, v_ref):
    z = jnp.dot(x_ref[...].astype(BF16), w_ref[...], preferred_element_type=F32)
    cos, sa, sb = cos_ref[...], sa_ref[...], sb_ref[...]

    def rope(t):
        return t * cos + pltpu.roll(t, LANES - 32, 1) * sa + pltpu.roll(t, 32, 1) * sb

    nq = N_HEADS * HEAD_DIM
    nk = N_KV_HEADS * HEAD_DIM
    for c in range(nq // LANES):
        sl = slice(c * LANES, (c + 1) * LANES)
        q_ref[:, sl] = (rope(z[:, sl]) * (HEAD_DIM ** -0.5)).astype(BF16)
    for c in range(nk // LANES):
        sl = slice(c * LANES, (c + 1) * LANES)
        k_ref[:, sl] = rope(z[:, nq + c * LANES:nq + (c + 1) * LANES]).astype(BF16)
    v_ref[...] = z[:, nq + nk:].astype(BF16)


def _qkv_rope(x, w, cos, sa, sb, seq, layer):
    t = x.shape[0]
    nq = N_HEADS * HEAD_DIM
    nk = N_KV_HEADS * HEAD_DIM
    tiles_per_seq = seq // TM_PROJ
    tab = pl.BlockSpec((TM_PROJ, LANES), lambda i: (i % tiles_per_seq, 0))
    row = lambda n: pl.BlockSpec((TM_PROJ, n), lambda i: (i, 0))
    return pl.pallas_call(
        _qkv_kernel,
        grid=(t // TM_PROJ,),
        in_specs=[row(D_MODEL), _resident((D_MODEL, QKV_COLS), layer), tab, tab, tab],
        out_specs=[row(nq), row(nk), row(nk)],
        out_shape=[jax.ShapeDtypeStruct((t, nq), BF16), jax.ShapeDtypeStruct((t, nk), BF16),
                   jax.ShapeDtypeStruct((t, nk), BF16)],
        compiler_params=_params(("parallel",), 32),
        name="qkv_rope",
    )(x, w, cos, sa, sb)


def _attn_kernel(sink_ref, q_ref, kc_ref, kp_ref, vc_ref, vp_ref, o_ref, *, tiles_per_seq):
    nblk = TQ_ATTN // WINDOW
    first = (pl.program_id(0) % tiles_per_seq) == 0
    kfull = jnp.concatenate([kp_ref[...], kc_ref[...]], axis=0)
    vfull = jnp.concatenate([vp_ref[...], vc_ref[...]], axis=0)

    qi = lax.broadcasted_iota(jnp.int32, (WINDOW, 2 * WINDOW), 0)
    kj = lax.broadcasted_iota(jnp.int32, (WINDOW, 2 * WINDOW), 1)
    band = (kj > qi) & (kj <= qi + WINDOW)
    band0 = band & (kj >= jnp.where(first, WINDOW, 0))
    col0 = kj[0:1, :] == 0
    lane_k = lax.broadcasted_iota(jnp.int32, (2 * WINDOW, LANES), 1)
    key0 = lax.broadcasted_iota(jnp.int32, (2 * WINDOW, LANES), 0) == 0
    lane_o = lax.broadcasted_iota(jnp.int32, (WINDOW, LANES), 1)
    ones = jnp.ones((2 * WINDOW, LANES), BF16)
    neg = jnp.finfo(F32).min

    for j in range(nblk):
        mask = band0 if j == 0 else band
        rows = slice(j * WINDOW, (j + 1) * WINDOW)
        keys = slice(j * WINDOW, (j + 2) * WINDOW)
        for kp in range(N_KV_HEADS // 2):
            pair = slice(kp * LANES, (kp + 1) * LANES)
            kblk = kfull[keys, pair]
            k_halves = (jnp.where(lane_k < HEAD_DIM, kblk, jnp.zeros_like(kblk)),
                        jnp.where(lane_k >= HEAD_DIM, kblk, jnp.zeros_like(kblk)))
            qstack = jnp.concatenate(
                [q_ref[rows, (kp * GROUP + c) * LANES:(kp * GROUP + c + 1) * LANES]
                 for c in range(GROUP)], axis=0)
            probs = []
            for half in range(2):
                s = lax.dot_general(qstack, k_halves[half], (((1,), (1,)), ((), ())),
                                    preferred_element_type=F32)
                for c in range(GROUP):
                    sink = sink_ref[(2 * kp + half) * GROUP + c]
                    fill = jnp.where(col0, sink, neg)
                    sc = jnp.where(mask, s[c * WINDOW:(c + 1) * WINDOW, :], fill)
                    m = jnp.max(sc, axis=-1, keepdims=True)
                    probs.append(jnp.exp(sc - m).astype(BF16))
            p = jnp.concatenate(probs, axis=0)
            vblk = vfull[keys, pair]
            vext = jnp.concatenate([jnp.where(key0, jnp.zeros_like(vblk), vblk), ones], axis=1)
            pv = jnp.dot(p, vext, preferred_element_type=F32)
            for c in range(GROUP):
                nums, dens = [], []
                for half in range(2):
                    r = (half * GROUP + c) * WINDOW
                    nums.append(pv[r:r + WINDOW, :LANES])
                    dens.append(pv[r:r + WINDOW, LANES:])
                low = lane_o < HEAD_DIM
                chunk = jnp.where(low, nums[0], nums[1]) / jnp.where(low, dens[0], dens[1])
                col = (kp * GROUP + c) * LANES
                o_ref[rows, col:col + LANES] = chunk.astype(BF16)


def _swa_attn(sinks, q, k, v, seq):
    t = q.shape[0]
    nq = N_HEADS * HEAD_DIM
    nk = N_KV_HEADS * HEAD_DIM
    blk_per_tile = TQ_ATTN // WINDOW
    cur = lambda n: pl.BlockSpec((TQ_ATTN, n), lambda i: (i, 0))
    prev = pl.BlockSpec((WINDOW, nk), lambda i: (jnp.maximum(i * blk_per_tile - 1, 0), 0))
    return pl.pallas_call(
        functools.partial(_attn_kernel, tiles_per_seq=seq // TQ_ATTN),
        grid=(t // TQ_ATTN,),
        in_specs=[pl.BlockSpec(memory_space=pltpu.SMEM), cur(nq), cur(nk), prev, cur(nk), prev],
        out_specs=cur(nq),
        out_shape=jax.ShapeDtypeStruct((t, nq), BF16),
        compiler_params=_params(("parallel",), 32),
        name="swa_attn",
    )(sinks, q, k, k, v, v)


def _lru_pieces(x_ref, win_ref, cw_ref, cb_ref, wra_ref, bra_ref, wrx_ref, brx_ref, lam_ref,
                wout_ref, g_ref, b_ref, out_ref, z_ref, hb_ref, xpad_ref, carry_ref):
    tm = TM_LRU
    half = D_RNN // 2
    z_ref[...] = jnp.dot(x_ref[...].astype(BF16), win_ref[...], preferred_element_type=F32)
    z_ref[:, D_RNN:] = jax.nn.gelu(z_ref[:, D_RNN:])
    yield

    for lanes in (slice(0, half), slice(half, D_RNN)):
        sub = lax.broadcasted_iota(jnp.int32, (1, SUBLANES, half), 1)
        cw = cw_ref[:, lanes]
        xb = z_ref[:, lanes]
        groups = jnp.concatenate([xpad_ref[:, lanes], xb], axis=0)
        groups = groups.reshape(tm // SUBLANES + 1, SUBLANES, half)
        xc = cb_ref[:, lanes] + cw[CONV_W - 1:CONV_W, :] * xb
        for d in range(1, CONV_W):
            rot = pltpu.roll(groups, d, 1)
            shifted = jnp.where(sub >= d, rot[1:], rot[:-1]).reshape(tm, half)
            xc = xc + cw[CONV_W - 1 - d:CONV_W - d, :] * shifted
        hb_ref[:, lanes] = xc
        xpad_ref[:, lanes] = xb[tm - SUBLANES:, :]
        yield

    for n in range(RNN_BLOCKS):
        sl = slice(n * RNN_BLOCK_W, (n + 1) * RNN_BLOCK_W)
        lam = lam_ref[:, sl]
        c_log_sig = LRU_C * (jnp.minimum(lam, 0.0) - jnp.log1p(jnp.exp(-jnp.abs(lam))))
        xn = hb_ref[:, sl]
        xr = xn.astype(BF16)
        r = _sigmoid(jnp.dot(xr, wra_ref[n], preferred_element_type=F32) + bra_ref[:, sl])
        i = _sigmoid(jnp.dot(xr, wrx_ref[n], preferred_element_type=F32) + brx_ref[:, sl])
        log_a = r * c_log_sig
        a = jnp.exp(log_a)
        z_ref[:, sl] = a
        q = -jnp.tanh(log_a) * (a * a + 1.0)
        root = jnp.where(q == 0.0, 0.0, q * lax.rsqrt(q))
        hb_ref[:, sl] = root * (i * xn)
        if n % 2 == 1:
            yield

    groups_per_piece = tm // SUBLANES // LRU_SCAN_PIECES
    for piece in range(LRU_SCAN_PIECES):
        row = lax.broadcasted_iota(jnp.int32, (SUBLANES, D_RNN), 0)
        h_in = carry_ref[...]
        for gi in range(piece * groups_per_piece, (piece + 1) * groups_per_piece):
            rows = slice(gi * SUBLANES, (gi + 1) * SUBLANES)
            a = z_ref[rows, :D_RNN]
            b = hb_ref[rows, :]
            for d in (1, 2, 4):
                a_sh = jnp.where(row >= d, pltpu.roll(a, d, 0), 1.0)
                b_sh = jnp.where(row >= d, pltpu.roll(b, d, 0), 0.0)
                b = a * b_sh + b
                a = a * a_sh
            h = a * h_in + b
            hb_ref[rows, :] = h
            h_in = h[SUBLANES - 1:SUBLANES, :]
        carry_ref[...] = h_in
        yield

    y = (hb_ref[...] * z_ref[:, D_RNN:]).astype(BF16)
    z_ref[:, :D_RNN] = ALPHA * x_ref[...] + jnp.dot(y, wout_ref[...], preferred_element_type=F32)
    yield
    out_ref[...] = _layernorm(z_ref[:, :D_RNN], g_ref[...], b_ref[...])


LRU_SCAN_PIECES = 4


N_LRU_REFS = 12


def _lru_ffn_kernel(*refs, tiles_per_seq, cast_next):
    lru_in, refs = refs[:N_LRU_REFS], refs[N_LRU_REFS:]
    wi_ref, wo_ref, g2_ref, b2_ref = refs[:4]
    if cast_next:
        nwi_ref, nwo_ref, o_ref, cwi_ref, cwo_ref = refs[4:9]
        scratch = refs[9:]
        _cast_slab((nwi_ref, nwo_ref), (cwi_ref, cwo_ref))
    else:
        o_ref = refs[4]
        scratch = refs[5:]
    mid_ref, xin_ref, act_ref, xb_ref, z_ref, hb_ref, xpad_ref, carry_ref = scratch
    step = pl.program_id(0)

    @pl.when(step == 0)
    def _():
        mid_ref[...] = jnp.zeros((TM_LRU, D_MODEL), F32)

    @pl.when(step % tiles_per_seq == 0)
    def _():
        xpad_ref[...] = jnp.zeros(xpad_ref.shape, F32)
        carry_ref[...] = jnp.zeros((1, D_RNN), F32)

    xin_ref[...] = mid_ref[...]
    pieces = _lru_pieces(*lru_in, mid_ref, z_ref, hb_ref, xpad_ref, carry_ref)
    _ffn_rows(xin_ref, slice(0, TM_LRU), wi_ref, wo_ref, g2_ref, b2_ref, o_ref, act_ref, xb_ref,
              interleave=pieces)
    for _ in pieces:
        pass


def _lru_ffn(x, lru_w, ln_g, ln_b, wi, wo, seq, layer, ln_lru, ln_ffn, next_w=None):
    t = x.shape[0]
    tiles = t // TM_LRU
    tile_in = pl.BlockSpec((TM_LRU, D_MODEL), lambda i: (jnp.minimum(i, tiles - 1), 0))
    tile_out = pl.BlockSpec((TM_LRU, D_MODEL), lambda i: (jnp.maximum(i - 1, 0), 0))
    vec = _resident((1, D_RNN), layer)
    gates = _resident((RNN_BLOCKS, RNN_BLOCK_W, RNN_BLOCK_W), layer)
    in_specs = [tile_in, _resident((D_MODEL, 2 * D_RNN), layer), _resident((CONV_W, D_RNN), layer),
                vec, gates, vec, gates, vec, vec, _resident((D_RNN, D_MODEL), layer),
                _resident((1, D_MODEL), ln_lru), _resident((1, D_MODEL), ln_lru),
                _resident((D_MODEL, 2 * D_FF)), _resident((D_FF, D_MODEL)),
                _resident((1, D_MODEL), ln_ffn), _resident((1, D_MODEL), ln_ffn)]
    assert len(in_specs) == N_LRU_REFS + 4
    out_specs, out_shape, args = [tile_out], [jax.ShapeDtypeStruct((t, D_MODEL), F32)], []
    if next_w is not None:
        cast_in, cast_out, cast_shape = _cast_specs(next_w[2], 16)
        in_specs += cast_in
        out_specs += cast_out
        out_shape += cast_shape
        args = list(next_w[:2])
    return pl.pallas_call(
        functools.partial(_lru_ffn_kernel, tiles_per_seq=seq // TM_LRU,
                          cast_next=next_w is not None),
        grid=(tiles + 1,),
        in_specs=in_specs,
        out_specs=out_specs,
        out_shape=out_shape,
        scratch_shapes=[pltpu.VMEM((TM_LRU, D_MODEL), F32),
                        pltpu.VMEM((TM_LRU, D_MODEL), F32),
                        pltpu.VMEM((TM_LRU, D_FF), BF16),
                        pltpu.VMEM((TM_LRU, D_MODEL), BF16),
                        pltpu.VMEM((TM_LRU, 2 * D_RNN), F32),
                        pltpu.VMEM((TM_LRU, D_RNN), F32),
                        pltpu.VMEM((SUBLANES, D_RNN), F32),
                        pltpu.VMEM((1, D_RNN), F32)],
        compiler_params=_params(("arbitrary",), 58),
        name="lru_ffn",
    )(x, *lru_w, ln_g, ln_b, wi, wo, ln_g, ln_b, *args)


def _rope_tables(seq):
    inv_freq = ROPE_THETA ** (-jnp.arange(0, HEAD_DIM, 2, dtype=F32) / HEAD_DIM)
    hi = jnp.arange(0, seq, ROPE_SPLIT, dtype=F32)[:, None] * inv_freq[None, :]
    lo = jnp.arange(ROPE_SPLIT, dtype=F32)[:, None] * inv_freq[None, :]
    ch, sh = jnp.cos(hi)[:, None, :], jnp.sin(hi)[:, None, :]
    cl, sl = jnp.cos(lo)[None, :, :], jnp.sin(lo)[None, :, :]
    cos = (ch * cl - sh * sl).reshape(seq, HEAD_DIM // 2)
    sin = (sh * cl + ch * sl).reshape(seq, HEAD_DIM // 2)
    zero = jnp.zeros_like(sin)
    cos_t = jnp.concatenate([cos, cos, cos, cos], axis=-1)
    sin_a = jnp.concatenate([-sin, zero, -sin, zero], axis=-1)
    sin_b = jnp.concatenate([zero, sin, zero, sin], axis=-1)
    return cos_t, sin_a, sin_b


def _permute_q_heads(w_qkv, w_o):
    nq = N_HEADS * HEAD_DIM
    n = w_qkv.shape[0]
    pairs = N_KV_HEADS // 2
    wq = w_qkv[:, :, :nq].reshape(n, D_MODEL, pairs, 2, GROUP, HEAD_DIM).swapaxes(3, 4)
    w_qkv_p = jnp.concatenate([wq.reshape(n, D_MODEL, nq), w_qkv[:, :, nq:]], axis=2)
    w_o_p = w_o.reshape(n, pairs, 2, GROUP, HEAD_DIM, D_MODEL).swapaxes(2, 3).reshape(n, nq, D_MODEL)
    return w_qkv_p, w_o_p


def kernel(x, ffn1_w_in, ffn1_w_out, ffn2_w_in, ffn2_w_out, ln_g, ln_b, attn_w_qkv, attn_sinks,
           attn_w_o, lru_w_in, lru_conv_w, lru_conv_b, lru_w_ra, lru_b_ra, lru_w_rx, lru_b_rx,
           lru_lambda, lru_w_out):
    batch, seq, _ = x.shape
    cos_t, sin_a, sin_b = _rope_tables(seq)
    bf = lambda a: a.astype(BF16)
    vecs = lambda a: a.reshape(-1, 1, a.shape[-1])
    w_qkv, w_o = (bf(w) for w in _permute_q_heads(attn_w_qkv, attn_w_o))
    lru = (bf(lru_w_in), lru_conv_w, vecs(lru_conv_b), bf(lru_w_ra), vecs(lru_b_ra),
           bf(lru_w_rx), vecs(lru_b_rx), vecs(lru_lambda), bf(lru_w_out))
    g, b = vecs(ln_g), vecs(ln_b)

    h = x.reshape(batch * seq, D_MODEL)
    wi, wo = bf(ffn1_w_in[0]), bf(ffn1_w_out[0])
    for i in range(DEPTH):
        h, wi, wo = _ffn_ln(h, wi, wo, g, b, 3 * i, next_w=(ffn2_w_in, ffn2_w_out, i))
        j = i // 2
        next_w = (ffn1_w_in, ffn1_w_out, i + 1) if i + 1 < DEPTH else None
        if i % 2 == 0:
            q, k, v = _qkv_rope(h, w_qkv, cos_t, sin_a, sin_b, seq, j)
            o = _swa_attn(attn_sinks[j], q, k, v, seq)
            outs = _ffn_ln(h, wi, wo, g, b, 3 * i + 2, next_w=next_w, proj=(o, w_o, j, 3 * i + 1))
        else:
            outs = _lru_ffn(h, lru, g, b, wi, wo, seq, j, 3 * i + 1, 3 * i + 2, next_w=next_w)
        h = outs[0]
        if next_w is not None:
            wi, wo = outs[1:]
    return h.reshape(batch, seq, D_MODEL)
```

```python
import functools

import jax
import jax.numpy as jnp
from jax import lax
from jax.experimental import pallas as pl
from jax.experimental.pallas import tpu as pltpu

F32 = jnp.float32
BF16 = jnp.bfloat16

D_MODEL = 1024
DEPTH = 4
N_HEADS = 16
N_KV_HEADS = 4
HEAD_DIM = 64
GROUP = N_HEADS // N_KV_HEADS
WINDOW = 128
ROPE_THETA = 10000.0
D_RNN = 1024
RNN_BLOCKS = 4
RNN_BLOCK_W = D_RNN // RNN_BLOCKS
CONV_W = 4
LRU_C = 8.0
D_FF = 2816
ALPHA = (2.0 * DEPTH) ** 0.25
LN_EPS = 1e-5
QKV_COLS = (N_HEADS + 2 * N_KV_HEADS) * HEAD_DIM

LANES = 128
SUBLANES = 8
MIB = 1024 * 1024

TM_FFN = 1024
SUB_FFN = 512
FF_CHUNKS = (256,) * 11
assert sum(FF_CHUNKS) == D_FF
TM_PROJ = 1024
ROPE_SPLIT = 64
TQ_ATTN = 1024
TM_LRU = 512


def _resident(shape, layer=None):
    nd = len(shape)
    if layer is None:
        return pl.BlockSpec(tuple(shape), lambda *_: (0,) * nd, pipeline_mode=pl.Buffered(1))
    return pl.BlockSpec((None,) + tuple(shape), lambda *_: (layer,) + (0,) * nd,
                        pipeline_mode=pl.Buffered(1))


def _sigmoid(x):
    return 0.5 * jnp.tanh(0.5 * x) + 0.5


def _params(semantics, vmem_mib):
    return pltpu.CompilerParams(dimension_semantics=semantics, vmem_limit_bytes=vmem_mib * MIB)


def _layernorm(y, g, b):
    mu = jnp.mean(y, axis=-1, keepdims=True)
    yc = y - mu
    var = jnp.mean(yc * yc, axis=-1, keepdims=True)
    return yc * lax.rsqrt(var + LN_EPS) * g + b


def _ffn_rows(x_ref, rows, wi_ref, wo_ref, g_ref, b_ref, o_ref, act_ref, xb_ref, interleave=None):
    xb_ref[rows, :] = x_ref[rows, :].astype(BF16)
    lo = 0
    for width in FF_CHUNKS:
        if interleave is not None:
            next(interleave, None)
        xb = xb_ref[rows, :]
        gate = jnp.dot(xb, wi_ref[:, lo:lo + width], preferred_element_type=F32)
        up = jnp.dot(xb, wi_ref[:, D_FF + lo:D_FF + lo + width], preferred_element_type=F32)
        half = 0.5 * gate
        act_ref[rows, lo:lo + width] = ((half * jnp.tanh(half) + half) * up).astype(BF16)
        lo += width
    down = jnp.dot(act_ref[rows, :], wo_ref[...], preferred_element_type=F32)
    y = ALPHA * x_ref[rows, :] + 0.5 * down
    o_ref[rows, :] = _layernorm(y, g_ref[...], b_ref[...])


def _cast_slab(src_refs, dst_refs):
    for src, dst in zip(src_refs, dst_refs):
        dst[...] = src[...].astype(BF16)


def _ffn_kernel(*refs, cast_next, proj):
    if proj:
        res_ref, a_ref, wp_ref, gp_ref, bp_ref = refs[:5]
        refs = refs[5:]
    else:
        x_ref, refs = refs[0], refs[1:]
    wi_ref, wo_ref, g_ref, b_ref = refs[:4]
    if cast_next:
        nwi_ref, nwo_ref, o_ref, cwi_ref, cwo_ref, act_ref, xb_ref = refs[4:]
        _cast_slab((nwi_ref, nwo_ref), (cwi_ref, cwo_ref))
    else:
        o_ref, act_ref, xb_ref = refs[4:]
    def build_input(rows):
        o_ref[rows, :] = ALPHA * res_ref[rows, :] + jnp.dot(a_ref[rows, :], wp_ref[...],
                                                            preferred_element_type=F32)
        yield
        o_ref[rows, :] = _layernorm(o_ref[rows, :], gp_ref[...], bp_ref[...])
        yield

    n_sub = TM_FFN // SUB_FFN
    sub_rows = [slice(s * SUB_FFN, (s + 1) * SUB_FFN) for s in range(n_sub)]
    if proj:
        x_ref = o_ref
        for _ in build_input(sub_rows[0]):
            pass
    for s in range(n_sub):
        ahead = build_input(sub_rows[s + 1]) if proj and s + 1 < n_sub else None
        _ffn_rows(x_ref, sub_rows[s], wi_ref, wo_ref, g_ref, b_ref, o_ref, act_ref, xb_ref,
                  interleave=ahead)


def _cast_specs(layer, slabs):
    ri, ro = D_MODEL // slabs, D_FF // slabs
    assert ri * slabs == D_MODEL and ro * slabs == D_FF and ri % 16 == 0 and ro % 16 == 0
    slab = lambda i: jnp.minimum(i, slabs - 1)
    in_specs = [pl.BlockSpec((None, ri, 2 * D_FF), lambda i: (layer, slab(i), 0)),
                pl.BlockSpec((None, ro, D_MODEL), lambda i: (layer, slab(i), 0))]
    out_specs = [pl.BlockSpec((ri, 2 * D_FF), lambda i: (slab(i), 0)),
                 pl.BlockSpec((ro, D_MODEL), lambda i: (slab(i), 0))]
    out_shape = [jax.ShapeDtypeStruct((D_MODEL, 2 * D_FF), BF16),
                 jax.ShapeDtypeStruct((D_FF, D_MODEL), BF16)]
    return in_specs, out_specs, out_shape


def _ffn_ln(x, wi, wo, ln_g, ln_b, ln_idx, next_w=None, proj=None):
    t = x.shape[0]
    steps = t // TM_FFN
    tile = pl.BlockSpec((TM_FFN, D_MODEL), lambda i: (i, 0))
    in_specs, lead = [tile], [x]
    if proj is not None:
        a, wp, player, pidx = proj
        in_specs += [pl.BlockSpec((TM_FFN, a.shape[1]), lambda i: (i, 0)),
                     _resident(wp.shape[1:], player),
                     _resident((1, D_MODEL), pidx), _resident((1, D_MODEL), pidx)]
        lead += [a, wp, ln_g, ln_b]
    in_specs += [_resident((D_MODEL, 2 * D_FF)), _resident((D_FF, D_MODEL)),
                 _resident((1, D_MODEL), ln_idx), _resident((1, D_MODEL), ln_idx)]
    out_specs, out_shape, args = [tile], [jax.ShapeDtypeStruct((t, D_MODEL), F32)], []
    if next_w is not None:
        cast_in, cast_out, cast_shape = _cast_specs(next_w[2], steps)
        in_specs += cast_in
        out_specs += cast_out
        out_shape += cast_shape
        args = list(next_w[:2])
    return pl.pallas_call(
        functools.partial(_ffn_kernel, cast_next=next_w is not None, proj=proj is not None),
        grid=(steps,),
        in_specs=in_specs,
        out_specs=out_specs,
        out_shape=out_shape,
        scratch_shapes=[pltpu.VMEM((TM_FFN, D_FF), BF16), pltpu.VMEM((TM_FFN, D_MODEL), BF16)],
        compiler_params=_params(("parallel",), 58),
        name="ffn_ln",
    )(*lead, wi, wo, ln_g, ln_b, *args)


def _qkv_kernel(x_ref, w_ref, cos_ref, sa_ref, sb_ref, q_ref, k_ref, v_ref):
    z = jnp.dot(x_ref[...].astype(BF16), w_ref[...], preferred_element_type=F32)
    cos, sa, sb = cos_ref[...], sa_ref[...], sb_ref[...]

    def rope(t):
        return t * cos + pltpu.roll(t, LANES - 32, 1) * sa + pltpu.roll(t, 32, 1) * sb

    nq = N_HEADS * HEAD_DIM
    nk = N_KV_HEADS * HEAD_DIM
    for c in range(nq // LANES):
        sl = slice(c * LANES, (c + 1) * LANES)
        q_ref[:, sl] = (rope(z[:, sl]) * (HEAD_DIM ** -0.5)).astype(BF16)
    for c in range(nk // LANES):
        sl = slice(c * LANES, (c + 1) * LANES)
        k_ref[:, sl] = rope(z[:, nq + c * LANES:nq + (c + 1) * LANES]).astype(BF16)
    v_ref[...] = z[:, nq + nk:].astype(BF16)


def _qkv_rope(x, w, cos, sa, sb, seq, layer):
    t = x.shape[0]
    nq = N_HEADS * HEAD_DIM
    nk = N_KV_HEADS * HEAD_DIM
    tiles_per_seq = seq // TM_PROJ
    tab = pl.BlockSpec((TM_PROJ, LANES), lambda i: (i % tiles_per_seq, 0))
    row = lambda n: pl.BlockSpec((TM_PROJ, n), lambda i: (i, 0))
    return pl.pallas_call(
        _qkv_kernel,
        grid=(t // TM_PROJ,),
        in_specs=[row(D_MODEL), _resident((D_MODEL, QKV_COLS), layer), tab, tab, tab],
        out_specs=[row(nq), row(nk), row(nk)],
        out_shape=[jax.ShapeDtypeStruct((t, nq), BF16), jax.ShapeDtypeStruct((t, nk), BF16),
                   jax.ShapeDtypeStruct((t, nk), BF16)],
        compiler_params=_params(("parallel",), 32),
        name="qkv_rope",
    )(x, w, cos, sa, sb)


def _attn_kernel(sink_ref, q_ref, kc_ref, kp_ref, vc_ref, vp_ref, o_ref, *, tiles_per_seq):
    nblk = TQ_ATTN // WINDOW
    first = (pl.program_id(0) % tiles_per_seq) == 0
    kfull = jnp.concatenate([kp_ref[...], kc_ref[...]], axis=0)
    vfull = jnp.concatenate([vp_ref[...], vc_ref[...]], axis=0)

    qi = lax.broadcasted_iota(jnp.int32, (WINDOW, 2 * WINDOW), 0)
    kj = lax.broadcasted_iota(jnp.int32, (WINDOW, 2 * WINDOW), 1)
    band = (kj > qi) & (kj <= qi + WINDOW)
    band0 = band & (kj >= jnp.where(first, WINDOW, 0))
    col0 = kj[0:1, :] == 0
    lane_k = lax.broadcasted_iota(jnp.int32, (2 * WINDOW, LANES), 1)
    key0 = lax.broadcasted_iota(jnp.int32, (2 * WINDOW, LANES), 0) == 0
    lane_o = lax.broadcasted_iota(jnp.int32, (WINDOW, LANES), 1)
    ones = jnp.ones((2 * WINDOW, LANES), BF16)
    neg = jnp.finfo(F32).min

    for j in range(nblk):
        mask = band0 if j == 0 else band
        rows = slice(j * WINDOW, (j + 1) * WINDOW)
        keys = slice(j * WINDOW, (j + 2) * WINDOW)
        for kp in range(N_KV_HEADS // 2):
            pair = slice(kp * LANES, (kp + 1) * LANES)
            kblk = kfull[keys, pair]
            k_halves = (jnp.where(lane_k < HEAD_DIM, kblk, jnp.zeros_like(kblk)),
                        jnp.where(lane_k >= HEAD_DIM, kblk, jnp.zeros_like(kblk)))
            qstack = jnp.concatenate(
                [q_ref[rows, (kp * GROUP + c) * LANES:(kp * GROUP + c + 1) * LANES]
                 for c in range(GROUP)], axis=0)
            probs = []
            for half in range(2):
                s = lax.dot_general(qstack, k_halves[half], (((1,), (1,)), ((), ())),
                                    preferred_element_type=F32)
                for c in range(GROUP):
                    sink = sink_ref[(2 * kp + half) * GROUP + c]
                    fill = jnp.where(col0, sink, neg)
                    sc = jnp.where(mask, s[c * WINDOW:(c + 1) * WINDOW, :], fill)
                    m = jnp.max(sc, axis=-1, keepdims=True)
                    probs.append(jnp.exp(sc - m).astype(BF16))
            p = jnp.concatenate(probs, axis=0)
            vblk = vfull[keys, pair]
            vext = jnp.concatenate([jnp.where(key0, jnp.zeros_like(vblk), vblk), ones], axis=1)
            pv = jnp.dot(p, vext, preferred_element_type=F32)
            for c in range(GROUP):
                nums, dens = [], []
                for half in range(2):
                    r = (half * GROUP + c) * WINDOW
                    nums.append(pv[r:r + WINDOW, :LANES])
                    dens.append(pv[r:r + WINDOW, LANES:])
                low = lane_o < HEAD_DIM
                chunk = jnp.where(low, nums[0], nums[1]) / jnp.where(low, dens[0], dens[1])
                col = (kp * GROUP + c) * LANES
                o_ref[rows, col:col + LANES] = chunk.astype(BF16)


def _swa_attn(sinks, q, k, v, seq):
    t = q.shape[0]
    nq = N_HEADS * HEAD_DIM
    nk = N_KV_HEADS * HEAD_DIM
    blk_per_tile = TQ_ATTN // WINDOW
    cur = lambda n: pl.BlockSpec((TQ_ATTN, n), lambda i: (i, 0))
    prev = pl.BlockSpec((WINDOW, nk), lambda i: (jnp.maximum(i * blk_per_tile - 1, 0), 0))
    return pl.pallas_call(
        functools.partial(_attn_kernel, tiles_per_seq=seq // TQ_ATTN),
        grid=(t // TQ_ATTN,),
        in_specs=[pl.BlockSpec(memory_space=pltpu.SMEM), cur(nq), cur(nk), prev, cur(nk), prev],
        out_specs=cur(nq),
        out_shape=jax.ShapeDtypeStruct((t, nq), BF16),
        compiler_params=_params(("parallel",), 32),
        name="swa_attn",
    )(sinks, q, k, k, v, v)


def _lru_pieces(x_ref, win_ref, cw_ref, cb_ref, wra_ref, bra_ref, wrx_ref, brx_ref, lam_ref,
                wout_ref, g_ref, b_ref, out_ref, z_ref, hb_ref, xpad_ref, carry_ref):
    tm = TM_LRU
    half = D_RNN // 2
    z_ref[...] = jnp.dot(x_ref[...].astype(BF16), win_ref[...], preferred_element_type=F32)
    z_ref[:, D_RNN:] = jax.nn.gelu(z_ref[:, D_RNN:])
    yield

    for lanes in (slice(0, half), slice(half, D_RNN)):
        sub = lax.broadcasted_iota(jnp.int32, (1, SUBLANES, half), 1)
        cw = cw_ref[:, lanes]
        xb = z_ref[:, lanes]
        groups = jnp.concatenate([xpad_ref[:, lanes], xb], axis=0)
        groups = groups.reshape(tm // SUBLANES + 1, SUBLANES, half)
        xc = cb_ref[:, lanes] + cw[CONV_W - 1:CONV_W, :] * xb
        for d in range(1, CONV_W):
            rot = pltpu.roll(groups, d, 1)
            shifted = jnp.where(sub >= d, rot[1:], rot[:-1]).reshape(tm, half)
            xc = xc + cw[CONV_W - 1 - d:CONV_W - d, :] * shifted
        hb_ref[:, lanes] = xc
        xpad_ref[:, lanes] = xb[tm - SUBLANES:, :]
        yield

    for n in range(RNN_BLOCKS):
        sl = slice(n * RNN_BLOCK_W, (n + 1) * RNN_BLOCK_W)
        lam = lam_ref[:, sl]
        c_log_sig = LRU_C * (jnp.minimum(lam, 0.0) - jnp.log1p(jnp.exp(-jnp.abs(lam))))
        xn = hb_ref[:, sl]
        xr = xn.astype(BF16)
        r = _sigmoid(jnp.dot(xr, wra_ref[n], preferred_element_type=F32) + bra_ref[:, sl])
        i = _sigmoid(jnp.dot(xr, wrx_ref[n], preferred_element_type=F32) + brx_ref[:, sl])
        log_a = r * c_log_sig
        a = jnp.exp(log_a)
        z_ref[:, sl] = a
        q = -jnp.tanh(log_a) * (a * a + 1.0)
        root = jnp.where(q == 0.0, 0.0, q * lax.rsqrt(q))
        hb_ref[:, sl] = root * (i * xn)
        if n % 2 == 1:
            yield

    groups_per_piece = tm // SUBLANES // LRU_SCAN_PIECES
    for piece in range(LRU_SCAN_PIECES):
        row = lax.broadcasted_iota(jnp.int32, (SUBLANES, D_RNN), 0)
        h_in = carry_ref[...]
        for gi in range(piece * groups_per_piece, (piece + 1) * groups_per_piece):
            rows = slice(gi * SUBLANES, (gi + 1) * SUBLANES)
            a = z_ref[rows, :D_RNN]
            b = hb_ref[rows, :]
            for d in (1, 2, 4):
                a_sh = jnp.where(row >= d, pltpu.roll(a, d, 0), 1.0)
                b_sh = jnp.where(row >= d, pltpu.roll(b, d, 0), 0.0)
                b = a * b_sh + b
                a = a * a_sh
            h = a * h_in + b
            hb_ref[rows, :] = h
            h_in = h[SUBLANES - 1:SUBLANES, :]
        carry_ref[...] = h_in
        yield

    y = (hb_ref[...] * z_ref[:, D_RNN:]).astype(BF16)
    z_ref[:, :D_RNN] = ALPHA * x_ref[...] + jnp.dot(y, wout_ref[...], preferred_element_type=F32)
    yield
    out_ref[...] = _layernorm(z_ref[:, :D_RNN], g_ref[...], b_ref[...])


LRU_SCAN_PIECES = 4


N_LRU_REFS = 12


def _lru_ffn_kernel(*refs, tiles_per_seq, cast_next):
    lru_in, refs = refs[:N_LRU_REFS], refs[N_LRU_REFS:]
    wi_ref, wo_ref, g2_ref, b2_ref = refs[:4]
    if cast_next:
        nwi_ref, nwo_ref, o_ref, cwi_ref, cwo_ref = refs[4:9]
        scratch = refs[9:]
        _cast_slab((nwi_ref, nwo_ref), (cwi_ref, cwo_ref))
    else:
        o_ref = refs[4]
        scratch = refs[5:]
    mid_ref, xin_ref, act_ref, xb_ref, z_ref, hb_ref, xpad_ref, carry_ref = scratch
    step = pl.program_id(0)

    @pl.when(step == 0)
    def _():
        mid_ref[...] = jnp.zeros((TM_LRU, D_MODEL), F32)

    @pl.when(step % tiles_per_seq == 0)
    def _():
        xpad_ref[...] = jnp.zeros(xpad_ref.shape, F32)
        carry_ref[...] = jnp.zeros((1, D_RNN), F32)

    xin_ref[...] = mid_ref[...]
    pieces = _lru_pieces(*lru_in, mid_ref, z_ref, hb_ref, xpad_ref, carry_ref)
    _ffn_rows(xin_ref, slice(0, TM_LRU), wi_ref, wo_ref, g2_ref, b2_ref, o_ref, act_ref, xb_ref,
              interleave=pieces)
    for _ in pieces:
        pass


def _lru_ffn(x, lru_w, ln_g, ln_b, wi, wo, seq, layer, ln_lru, ln_ffn, next_w=None):
    t = x.shape[0]
    tiles = t // TM_LRU
    tile_in = pl.BlockSpec((TM_LRU, D_MODEL), lambda i: (jnp.minimum(i, tiles - 1), 0))
    tile_out = pl.BlockSpec((TM_LRU, D_MODEL), lambda i: (jnp.maximum(i - 1, 0), 0))
    vec = _resident((1, D_RNN), layer)
    gates = _resident((RNN_BLOCKS, RNN_BLOCK_W, RNN_BLOCK_W), layer)
    in_specs = [tile_in, _resident((D_MODEL, 2 * D_RNN), layer), _resident((CONV_W, D_RNN), layer),
                vec, gates, vec, gates, vec, vec, _resident((D_RNN, D_MODEL), layer),
                _resident((1, D_MODEL), ln_lru), _resident((1, D_MODEL), ln_lru),
                _resident((D_MODEL, 2 * D_FF)), _resident((D_FF, D_MODEL)),
                _resident((1, D_MODEL), ln_ffn), _resident((1, D_MODEL), ln_ffn)]
    assert len(in_specs) == N_LRU_REFS + 4
    out_specs, out_shape, args = [tile_out], [jax.ShapeDtypeStruct((t, D_MODEL), F32)], []
    if next_w is not None:
        cast_in, cast_out, cast_shape = _cast_specs(next_w[2], 16)
        in_specs += cast_in
        out_specs += cast_out
        out_shape += cast_shape
        args = list(next_w[:2])
    return pl.pallas_call(
        functools.partial(_lru_ffn_kernel, tiles_per_seq=seq // TM_LRU,
                          cast_next=next_w is not None),
        grid=(tiles + 1,),
        in_specs=in_specs,
        out_specs=out_specs,
        out_shape=out_shape,
        scratch_shapes=[pltpu.VMEM((TM_LRU, D_MODEL), F32),
                        pltpu.VMEM((TM_LRU, D_MODEL), F32),
                        pltpu.VMEM((TM_LRU, D_FF), BF16),
                        pltpu.VMEM((TM_LRU, D_MODEL), BF16),
                        pltpu.VMEM((TM_LRU, 2 * D_RNN), F32),
                        pltpu.VMEM((TM_LRU, D_RNN), F32),
                        pltpu.VMEM((SUBLANES, D_RNN), F32),
                        pltpu.VMEM((1, D_RNN), F32)],
        compiler_params=_params(("arbitrary",), 58),
        name="lru_ffn",
    )(x, *lru_w, ln_g, ln_b, wi, wo, ln_g, ln_b, *args)


def _rope_tables(seq):
    inv_freq = ROPE_THETA ** (-jnp.arange(0, HEAD_DIM, 2, dtype=F32) / HEAD_DIM)
    hi = jnp.arange(0, seq, ROPE_SPLIT, dtype=F32)[:, None] * inv_freq[None, :]
    lo = jnp.arange(ROPE_SPLIT, dtype=F32)[:, None] * inv_freq[None, :]
    ch, sh = jnp.cos(hi)[:, None, :], jnp.sin(hi)[:, None, :]
    cl, sl = jnp.cos(lo)[None, :, :], jnp.sin(lo)[None, :, :]
    cos = (ch * cl - sh * sl).reshape(seq, HEAD_DIM // 2)
    sin = (sh * cl + ch * sl).reshape(seq, HEAD_DIM // 2)
    zero = jnp.zeros_like(sin)
    cos_t = jnp.concatenate([cos, cos, cos, cos], axis=-1)
    sin_a = jnp.concatenate([-sin, zero, -sin, zero], axis=-1)
    sin_b = jnp.concatenate([zero, sin, zero, sin], axis=-1)
    return cos_t, sin_a, sin_b


def _permute_q_heads(w_qkv, w_o):
    nq = N_HEADS * HEAD_DIM
    n = w_qkv.shape[0]
    pairs = N_KV_HEADS // 2
    wq = w_qkv[:, :, :nq].reshape(n, D_MODEL, pairs, 2, GROUP, HEAD_DIM).swapaxes(3, 4)
    w_qkv_p = jnp.concatenate([wq.reshape(n, D_MODEL, nq), w_qkv[:, :, nq:]], axis=2)
    w_o_p = w_o.reshape(n, pairs, 2, GROUP, HEAD_DIM, D_MODEL).swapaxes(2, 3).reshape(n, nq, D_MODEL)
    return w_qkv_p, w_o_p


def kernel(x, ffn1_w_in, ffn1_w_out, ffn2_w_in, ffn2_w_out, ln_g, ln_b, attn_w_qkv, attn_sinks,
           attn_w_o, lru_w_in, lru_conv_w, lru_conv_b, lru_w_ra, lru_b_ra, lru_w_rx, lru_b_rx,
           lru_lambda, lru_w_out):
    batch, seq, _ = x.shape
    cos_t, sin_a, sin_b = _rope_tables(seq)
    bf = lambda a: a.astype(BF16)
    vecs = lambda a: a.reshape(-1, 1, a.shape[-1])
    w_qkv, w_o = (bf(w) for w in _permute_q_heads(attn_w_qkv, attn_w_o))
    lru = (bf(lru_w_in), lru_conv_w, vecs(lru_conv_b), bf(lru_w_ra), vecs(lru_b_ra),
           bf(lru_w_rx), vecs(lru_b_rx), vecs(lru_lambda), bf(lru_w_out))
    g, b = vecs(ln_g), vecs(ln_b)

    h = x.reshape(batch * seq, D_MODEL)
    wi, wo = bf(ffn1_w_in[0]), bf(ffn1_w_out[0])
    for i in range(DEPTH):
        h, wi, wo = _ffn_ln(h, wi, wo, g, b, 3 * i, next_w=(ffn2_w_in, ffn2_w_out, i))
        j = i // 2
        next_w = (ffn1_w_in, ffn1_w_out, i + 1) if i + 1 < DEPTH else None
        if i % 2 == 0:
            q, k, v = _qkv_rope(h, w_qkv, cos_t, sin_a, sin_b, seq, j)
            o = _swa_attn(attn_sinks[j], q, k, v, seq)
            outs = _ffn_ln(h, wi, wo, g, b, 3 * i + 2, next_w=next_w, proj=(o, w_o, j, 3 * i + 1))
        else:
            outs = _lru_ffn(h, lru, g, b, wi, wo, seq, j, 3 * i + 1, 3 * i + 2, next_w=next_w)
        h = outs[0]
        if next_w is not None:
            wi, wo = outs[1:]
    return h.reshape(batch, seq, D_MODEL)
```

```python
import functools

import jax
import jax.numpy as jnp
from jax import lax
from jax.experimental import pallas as pl
from jax.experimental.pallas import tpu as pltpu

F32 = jnp.float32
BF16 = jnp.bfloat16

D_MODEL = 1024
DEPTH = 4
N_HEADS = 16
N_KV_HEADS = 4
HEAD_DIM = 64
GROUP = N_HEADS // N_KV_HEADS
WINDOW = 128
ROPE_THETA = 10000.0
D_RNN = 1024
RNN_BLOCKS = 4
RNN_BLOCK_W = D_RNN // RNN_BLOCKS
CONV_W = 4
LRU_C = 8.0
D_FF = 2816
ALPHA = (2.0 * DEPTH) ** 0.25
LN_EPS = 1e-5
QKV_COLS = (N_HEADS + 2 * N_KV_HEADS) * HEAD_DIM

LANES = 128
SUBLANES = 8
MIB = 1024 * 1024

TM_FFN = 1024
SUB_FFN = 512
FF_CHUNKS = (256,) * 11
assert sum(FF_CHUNKS) == D_FF
TM_PROJ = 2048
ROPE_SPLIT = 64
TQ_ATTN = 2048
TM_LRU = 512


def _resident(shape, layer=None):
    nd = len(shape)
    if layer is None:
        return pl.BlockSpec(tuple(shape), lambda *_: (0,) * nd, pipeline_mode=pl.Buffered(1))
    return pl.BlockSpec((None,) + tuple(shape), lambda *_: (layer,) + (0,) * nd,
                        pipeline_mode=pl.Buffered(1))


def _sigmoid(x):
    return 0.5 * jnp.tanh(0.5 * x) + 0.5


def _params(semantics, vmem_mib):
    return pltpu.CompilerParams(dimension_semantics=semantics, vmem_limit_bytes=vmem_mib * MIB)


def _layernorm(y, g, b):
    mu = jnp.mean(y, axis=-1, keepdims=True)
    yc = y - mu
    var = jnp.mean(yc * yc, axis=-1, keepdims=True)
    return yc * lax.rsqrt(var + LN_EPS) * g + b


def _ffn_rows(x_ref, rows, wi_ref, wo_ref, g_ref, b_ref, o_ref, act_ref, xb_ref, interleave=None):
    xb_ref[rows, :] = x_ref[rows, :].astype(BF16)
    lo = 0
    for width in FF_CHUNKS:
        if interleave is not None:
            next(interleave, None)
        xb = xb_ref[rows, :]
        gate = jnp.dot(xb, wi_ref[:, lo:lo + width], preferred_element_type=F32)
        up = jnp.dot(xb, wi_ref[:, D_FF + lo:D_FF + lo + width], preferred_element_type=F32)
        half = 0.5 * gate
        act_ref[rows, lo:lo + width] = ((half * jnp.tanh(half) + half) * up).astype(BF16)
        lo += width
    down = jnp.dot(act_ref[rows, :], wo_ref[...], preferred_element_type=F32)
    y = ALPHA * x_ref[rows, :] + 0.5 * down
    o_ref[rows, :] = _layernorm(y, g_ref[...], b_ref[...])


def _cast_slab(src_refs, dst_refs):
    for src, dst in zip(src_refs, dst_refs):
        dst[...] = src[...].astype(BF16)


def _ffn_kernel(*refs, cast_next, proj):
    if proj:
        res_ref, a_ref, wp_ref, gp_ref, bp_ref = refs[:5]
        refs = refs[5:]
    else:
        x_ref, refs = refs[0], refs[1:]
    wi_ref, wo_ref, g_ref, b_ref = refs[:4]
    if cast_next:
        nwi_ref, nwo_ref, o_ref, cwi_ref, cwo_ref, act_ref, xb_ref = refs[4:]
        _cast_slab((nwi_ref, nwo_ref), (cwi_ref, cwo_ref))
    else:
        o_ref, act_ref, xb_ref = refs[4:]
    def build_input(rows):
        o_ref[rows, :] = ALPHA * res_ref[rows, :] + jnp.dot(a_ref[rows, :], wp_ref[...],
                                                            preferred_element_type=F32)
        yield
        o_ref[rows, :] = _layernorm(o_ref[rows, :], gp_ref[...], bp_ref[...])
        yield

    n_sub = TM_FFN // SUB_FFN
    sub_rows = [slice(s * SUB_FFN, (s + 1) * SUB_FFN) for s in range(n_sub)]
    if proj:
        x_ref = o_ref
        for _ in build_input(sub_rows[0]):
            pass
    for s in range(n_sub):
        ahead = build_input(sub_rows[s + 1]) if proj and s + 1 < n_sub else None
        _ffn_rows(x_ref, sub_rows[s], wi_ref, wo_ref, g_ref, b_ref, o_ref, act_ref, xb_ref,
                  interleave=ahead)


def _cast_specs(layer, slabs):
    ri, ro = D_MODEL // slabs, D_FF // slabs
    assert ri * slabs == D_MODEL and ro * slabs == D_FF and ri % 16 == 0 and ro % 16 == 0
    slab = lambda i: jnp.minimum(i, slabs - 1)
    in_specs = [pl.BlockSpec((None, ri, 2 * D_FF), lambda i: (layer, slab(i), 0)),
                pl.BlockSpec((None, ro, D_MODEL), lambda i: (layer, slab(i), 0))]
    out_specs = [pl.BlockSpec((ri, 2 * D_FF), lambda i: (slab(i), 0)),
                 pl.BlockSpec((ro, D_MODEL), lambda i: (slab(i), 0))]
    out_shape = [jax.ShapeDtypeStruct((D_MODEL, 2 * D_FF), BF16),
                 jax.ShapeDtypeStruct((D_FF, D_MODEL), BF16)]
    return in_specs, out_specs, out_shape


def _ffn_ln(x, wi, wo, ln_g, ln_b, ln_idx, next_w=None, proj=None):
    t = x.shape[0]
    steps = t // TM_FFN
    tile = pl.BlockSpec((TM_FFN, D_MODEL), lambda i: (i, 0))
    in_specs, lead = [tile], [x]
    if proj is not None:
        a, wp, player, pidx = proj
        in_specs += [pl.BlockSpec((TM_FFN, a.shape[1]), lambda i: (i, 0)),
                     _resident(wp.shape[1:], player),
                     _resident((1, D_MODEL), pidx), _resident((1, D_MODEL), pidx)]
        lead += [a, wp, ln_g, ln_b]
    in_specs += [_resident((D_MODEL, 2 * D_FF)), _resident((D_FF, D_MODEL)),
                 _resident((1, D_MODEL), ln_idx), _resident((1, D_MODEL), ln_idx)]
    out_specs, out_shape, args = [tile], [jax.ShapeDtypeStruct((t, D_MODEL), F32)], []
    if next_w is not None:
        cast_in, cast_out, cast_shape = _cast_specs(next_w[2], steps)
        in_specs += cast_in
        out_specs += cast_out
        out_shape += cast_shape
        args = list(next_w[:2])
    return pl.pallas_call(
        functools.partial(_ffn_kernel, cast_next=next_w is not None, proj=proj is not None),
        grid=(steps,),
        in_specs=in_specs,
        out_specs=out_specs,
        out_shape=out_shape,
        scratch_shapes=[pltpu.VMEM((TM_FFN, D_FF), BF16), pltpu.VMEM((TM_FFN, D_MODEL), BF16)],
        compiler_params=_params(("parallel",), 58),
        name="ffn_ln",
    )(*lead, wi, wo, ln_g, ln_b, *args)


def _qkv_kernel(x_ref, w_ref, cos_ref, sa_ref, sb_ref, q_ref, k_ref, v_ref):
    z = jnp.dot(x_ref[...].astype(BF16), w_ref[...], preferred_element_type=F32)
    cos, sa, sb = cos_ref[...], sa_ref[...], sb_ref[...]

    def rope(t):
        return t * cos + pltpu.roll(t, LANES - 32, 1) * sa + pltpu.roll(t, 32, 1) * sb

    nq = N_HEADS * HEAD_DIM
    nk = N_KV_HEADS * HEAD_DIM
    for c in range(nq // LANES):
        sl = slice(c * LANES, (c + 1) * LANES)
        q_ref[:, sl] = (rope(z[:, sl]) * (HEAD_DIM ** -0.5)).astype(BF16)
    for c in range(nk // LANES):
        sl = slice(c * LANES, (c + 1) * LANES)
        k_ref[:, sl] = rope(z[:, nq + c * LANES:nq + (c + 1) * LANES]).astype(BF16)
    v_ref[...] = z[:, nq + nk:].astype(BF16)


def _qkv_rope(x, w, cos, sa, sb, seq, layer):
    t = x.shape[0]
    nq = N_HEADS * HEAD_DIM
    nk = N_KV_HEADS * HEAD_DIM
    tiles_per_seq = seq // TM_PROJ
    tab = pl.BlockSpec((TM_PROJ, LANES), lambda i: (i % tiles_per_seq, 0))
    row = lambda n: pl.BlockSpec((TM_PROJ, n), lambda i: (i, 0))
    return pl.pallas_call(
        _qkv_kernel,
        grid=(t // TM_PROJ,),
        in_specs=[row(D_MODEL), _resident((D_MODEL, QKV_COLS), layer), tab, tab, tab],
        out_specs=[row(nq), row(nk), row(nk)],
        out_shape=[jax.ShapeDtypeStruct((t, nq), BF16), jax.ShapeDtypeStruct((t, nk), BF16),
                   jax.ShapeDtypeStruct((t, nk), BF16)],
        compiler_params=_params(("parallel",), 56),
        name="qkv_rope",
    )(x, w, cos, sa, sb)


def _attn_kernel(sink_ref, q_ref, kc_ref, kp_ref, vc_ref, vp_ref, o_ref, *, tiles_per_seq):
    nblk = TQ_ATTN // WINDOW
    first = (pl.program_id(0) % tiles_per_seq) == 0
    kfull = jnp.concatenate([kp_ref[...], kc_ref[...]], axis=0)
    vfull = jnp.concatenate([vp_ref[...], vc_ref[...]], axis=0)

    qi = lax.broadcasted_iota(jnp.int32, (WINDOW, 2 * WINDOW), 0)
    kj = lax.broadcasted_iota(jnp.int32, (WINDOW, 2 * WINDOW), 1)
    band = (kj > qi) & (kj <= qi + WINDOW)
    band0 = band & (kj >= jnp.where(first, WINDOW, 0))
    col0 = kj[0:1, :] == 0
    lane_k = lax.broadcasted_iota(jnp.int32, (2 * WINDOW, LANES), 1)
    key0 = lax.broadcasted_iota(jnp.int32, (2 * WINDOW, LANES), 0) == 0
    lane_o = lax.broadcasted_iota(jnp.int32, (WINDOW, LANES), 1)
    ones = jnp.ones((2 * WINDOW, LANES), BF16)
    neg = jnp.finfo(F32).min

    for j in range(nblk):
        mask = band0 if j == 0 else band
        rows = slice(j * WINDOW, (j + 1) * WINDOW)
        keys = slice(j * WINDOW, (j + 2) * WINDOW)
        for kp in range(N_KV_HEADS // 2):
            pair = slice(kp * LANES, (kp + 1) * LANES)
            kblk = kfull[keys, pair]
            k_halves = (jnp.where(lane_k < HEAD_DIM, kblk, jnp.zeros_like(kblk)),
                        jnp.where(lane_k >= HEAD_DIM, kblk, jnp.zeros_like(kblk)))
            qstack = jnp.concatenate(
                [q_ref[rows, (kp * GROUP + c) * LANES:(kp * GROUP + c + 1) * LANES]
                 for c in range(GROUP)], axis=0)
            probs = []
            for half in range(2):
                s = lax.dot_general(qstack, k_halves[half], (((1,), (1,)), ((), ())),
                                    preferred_element_type=F32)
                for c in range(GROUP):
                    sink = sink_ref[(2 * kp + half) * GROUP + c]
                    fill = jnp.where(col0, sink, neg)
                    sc = jnp.where(mask, s[c * WINDOW:(c + 1) * WINDOW, :], fill)
                    m = jnp.max(sc, axis=-1, keepdims=True)
                    probs.append(jnp.exp(sc - m).astype(BF16))
            p = jnp.concatenate(probs, axis=0)
            vblk = vfull[keys, pair]
            vext = jnp.concatenate([jnp.where(key0, jnp.zeros_like(vblk), vblk), ones], axis=1)
            pv = jnp.dot(p, vext, preferred_element_type=F32)
            for c in range(GROUP):
                nums, dens = [], []
                for half in range(2):
                    r = (half * GROUP + c) * WINDOW
                    nums.append(pv[r:r + WINDOW, :LANES])
                    dens.append(pv[r:r + WINDOW, LANES:])
                low = lane_o < HEAD_DIM
                chunk = jnp.where(low, nums[0], nums[1]) / jnp.where(low, dens[0], dens[1])
                col = (kp * GROUP + c) * LANES
                o_ref[rows, col:col + LANES] = chunk.astype(BF16)


def _swa_attn(sinks, q, k, v, seq):
    t = q.shape[0]
    nq = N_HEADS * HEAD_DIM
    nk = N_KV_HEADS * HEAD_DIM
    blk_per_tile = TQ_ATTN // WINDOW
    cur = lambda n: pl.BlockSpec((TQ_ATTN, n), lambda i: (i, 0))
    prev = pl.BlockSpec((WINDOW, nk), lambda i: (jnp.maximum(i * blk_per_tile - 1, 0), 0))
    return pl.pallas_call(
        functools.partial(_attn_kernel, tiles_per_seq=seq // TQ_ATTN),
        grid=(t // TQ_ATTN,),
        in_specs=[pl.BlockSpec(memory_space=pltpu.SMEM), cur(nq), cur(nk), prev, cur(nk), prev],
        out_specs=cur(nq),
        out_shape=jax.ShapeDtypeStruct((t, nq), BF16),
        compiler_params=_params(("parallel",), 32),
        name="swa_attn",
    )(sinks, q, k, k, v, v)


def _lru_pieces(x_ref, win_ref, cw_ref, cb_ref, wra_ref, bra_ref, wrx_ref, brx_ref, lam_ref,
                wout_ref, g_ref, b_ref, out_ref, z_ref, hb_ref, xpad_ref, carry_ref):
    tm = TM_LRU
    half = D_RNN // 2
    z_ref[...] = jnp.dot(x_ref[...].astype(BF16), win_ref[...], preferred_element_type=F32)
    z_ref[:, D_RNN:] = jax.nn.gelu(z_ref[:, D_RNN:])
    yield

    for lanes in (slice(0, half), slice(half, D_RNN)):
        sub = lax.broadcasted_iota(jnp.int32, (1, SUBLANES, half), 1)
        cw = cw_ref[:, lanes]
        xb = z_ref[:, lanes]
        groups = jnp.concatenate([xpad_ref[:, lanes], xb], axis=0)
        groups = groups.reshape(tm // SUBLANES + 1, SUBLANES, half)
        xc = cb_ref[:, lanes] + cw[CONV_W - 1:CONV_W, :] * xb
        for d in range(1, CONV_W):
            rot = pltpu.roll(groups, d, 1)
            shifted = jnp.where(sub >= d, rot[1:], rot[:-1]).reshape(tm, half)
            xc = xc + cw[CONV_W - 1 - d:CONV_W - d, :] * shifted
        hb_ref[:, lanes] = xc
        xpad_ref[:, lanes] = xb[tm - SUBLANES:, :]
        yield

    for n in range(RNN_BLOCKS):
        sl = slice(n * RNN_BLOCK_W, (n + 1) * RNN_BLOCK_W)
        lam = lam_ref[:, sl]
        c_log_sig = LRU_C * (jnp.minimum(lam, 0.0) - jnp.log1p(jnp.exp(-jnp.abs(lam))))
        xn = hb_ref[:, sl]
        xr = xn.astype(BF16)
        r = _sigmoid(jnp.dot(xr, wra_ref[n], preferred_element_type=F32) + bra_ref[:, sl])
        i = _sigmoid(jnp.dot(xr, wrx_ref[n], preferred_element_type=F32) + brx_ref[:, sl])
        log_a = r * c_log_sig
        a = jnp.exp(log_a)
        z_ref[:, sl] = a
        q = -jnp.tanh(log_a) * (a * a + 1.0)
        root = jnp.where(q == 0.0, 0.0, q * lax.rsqrt(q))
        hb_ref[:, sl] = root * (i * xn)
        if n % 2 == 1:
            yield

    groups_per_piece = tm // SUBLANES // LRU_SCAN_PIECES
    for piece in range(LRU_SCAN_PIECES):
        row = lax.broadcasted_iota(jnp.int32, (SUBLANES, D_RNN), 0)
        h_in = carry_ref[...]
        for gi in range(piece * groups_per_piece, (piece + 1) * groups_per_piece):
            rows = slice(gi * SUBLANES, (gi + 1) * SUBLANES)
            a = z_ref[rows, :D_RNN]
            b = hb_ref[rows, :]
            for d in (1, 2, 4):
                a_sh = jnp.where(row >= d, pltpu.roll(a, d, 0), 1.0)
                b_sh = jnp.where(row >= d, pltpu.roll(b, d, 0), 0.0)
                b = a * b_sh + b
                a = a * a_sh
            h = a * h_in + b
            hb_ref[rows, :] = h
            h_in = h[SUBLANES - 1:SUBLANES, :]
        carry_ref[...] = h_in
        yield

    y = (hb_ref[...] * z_ref[:, D_RNN:]).astype(BF16)
    z_ref[:, :D_RNN] = ALPHA * x_ref[...] + jnp.dot(y, wout_ref[...], preferred_element_type=F32)
    yield
    out_ref[...] = _layernorm(z_ref[:, :D_RNN], g_ref[...], b_ref[...])


LRU_SCAN_PIECES = 4


N_LRU_REFS = 12


def _lru_ffn_kernel(*refs, tiles_per_seq, cast_next):
    lru_in, refs = refs[:N_LRU_REFS], refs[N_LRU_REFS:]
    wi_ref, wo_ref, g2_ref, b2_ref = refs[:4]
    if cast_next:
        nwi_ref, nwo_ref, o_ref, cwi_ref, cwo_ref = refs[4:9]
        scratch = refs[9:]
        _cast_slab((nwi_ref, nwo_ref), (cwi_ref, cwo_ref))
    else:
        o_ref = refs[4]
        scratch = refs[5:]
    mid_ref, xin_ref, act_ref, xb_ref, z_ref, hb_ref, xpad_ref, carry_ref = scratch
    step = pl.program_id(0)

    @pl.when(step == 0)
    def _():
        mid_ref[...] = jnp.zeros((TM_LRU, D_MODEL), F32)

    @pl.when(step % tiles_per_seq == 0)
    def _():
        xpad_ref[...] = jnp.zeros(xpad_ref.shape, F32)
        carry_ref[...] = jnp.zeros((1, D_RNN), F32)

    xin_ref[...] = mid_ref[...]
    pieces = _lru_pieces(*lru_in, mid_ref, z_ref, hb_ref, xpad_ref, carry_ref)
    _ffn_rows(xin_ref, slice(0, TM_LRU), wi_ref, wo_ref, g2_ref, b2_ref, o_ref, act_ref, xb_ref,
              interleave=pieces)
    for _ in pieces:
        pass


def _lru_ffn(x, lru_w, ln_g, ln_b, wi, wo, seq, layer, ln_lru, ln_ffn, next_w=None):
    t = x.shape[0]
    tiles = t // TM_LRU
    tile_in = pl.BlockSpec((TM_LRU, D_MODEL), lambda i: (jnp.minimum(i, tiles - 1), 0))
    tile_out = pl.BlockSpec((TM_LRU, D_MODEL), lambda i: (jnp.maximum(i - 1, 0), 0))
    vec = _resident((1, D_RNN), layer)
    gates = _resident((RNN_BLOCKS, RNN_BLOCK_W, RNN_BLOCK_W), layer)
    in_specs = [tile_in, _resident((D_MODEL, 2 * D_RNN), layer), _resident((CONV_W, D_RNN), layer),
                vec, gates, vec, gates, vec, vec, _resident((D_RNN, D_MODEL), layer),
                _resident((1, D_MODEL), ln_lru), _resident((1, D_MODEL), ln_lru),
                _resident((D_MODEL, 2 * D_FF)), _resident((D_FF, D_MODEL)),
                _resident((1, D_MODEL), ln_ffn), _resident((1, D_MODEL), ln_ffn)]
    assert len(in_specs) == N_LRU_REFS + 4
    out_specs, out_shape, args = [tile_out], [jax.ShapeDtypeStruct((t, D_MODEL), F32)], []
    if next_w is not None:
        cast_in, cast_out, cast_shape = _cast_specs(next_w[2], 16)
        in_specs += cast_in
        out_specs += cast_out
        out_shape += cast_shape
        args = list(next_w[:2])
    return pl.pallas_call(
        functools.partial(_lru_ffn_kernel, tiles_per_seq=seq // TM_LRU,
                          cast_next=next_w is not None),
        grid=(tiles + 1,),
        in_specs=in_specs,
        out_specs=out_specs,
        out_shape=out_shape,
        scratch_shapes=[pltpu.VMEM((TM_LRU, D_MODEL), F32),
                        pltpu.VMEM((TM_LRU, D_MODEL), F32),
                        pltpu.VMEM((TM_LRU, D_FF), BF16),
                        pltpu.VMEM((TM_LRU, D_MODEL), BF16),
                        pltpu.VMEM((TM_LRU, 2 * D_RNN), F32),
                        pltpu.VMEM((TM_LRU, D_RNN), F32),
                        pltpu.VMEM((SUBLANES, D_RNN), F32),
                        pltpu.VMEM((1, D_RNN), F32)],
        compiler_params=_params(("arbitrary",), 58),
        name="lru_ffn",
    )(x, *lru_w, ln_g, ln_b, wi, wo, ln_g, ln_b, *args)


def _rope_tables(seq):
    inv_freq = ROPE_THETA ** (-jnp.arange(0, HEAD_DIM, 2, dtype=F32) / HEAD_DIM)
    hi = jnp.arange(0, seq, ROPE_SPLIT, dtype=F32)[:, None] * inv_freq[None, :]
    lo = jnp.arange(ROPE_SPLIT, dtype=F32)[:, None] * inv_freq[None, :]
    ch, sh = jnp.cos(hi)[:, None, :], jnp.sin(hi)[:, None, :]
    cl, sl = jnp.cos(lo)[None, :, :], jnp.sin(lo)[None, :, :]
    cos = (ch * cl - sh * sl).reshape(seq, HEAD_DIM // 2)
    sin = (sh * cl + ch * sl).reshape(seq, HEAD_DIM // 2)
    zero = jnp.zeros_like(sin)
    cos_t = jnp.concatenate([cos, cos, cos, cos], axis=-1)
    sin_a = jnp.concatenate([-sin, zero, -sin, zero], axis=-1)
    sin_b = jnp.concatenate([zero, sin, zero, sin], axis=-1)
    return cos_t, sin_a, sin_b


def _permute_q_heads(w_qkv, w_o):
    nq = N_HEADS * HEAD_DIM
    n = w_qkv.shape[0]
    pairs = N_KV_HEADS // 2
    wq = w_qkv[:, :, :nq].reshape(n, D_MODEL, pairs, 2, GROUP, HEAD_DIM).swapaxes(3, 4)
    w_qkv_p = jnp.concatenate([wq.reshape(n, D_MODEL, nq), w_qkv[:, :, nq:]], axis=2)
    w_o_p = w_o.reshape(n, pairs, 2, GROUP, HEAD_DIM, D_MODEL).swapaxes(2, 3).reshape(n, nq, D_MODEL)
    return w_qkv_p, w_o_p


def kernel(x, ffn1_w_in, ffn1_w_out, ffn2_w_in, ffn2_w_out, ln_g, ln_b, attn_w_qkv, attn_sinks,
           attn_w_o, lru_w_in, lru_conv_w, lru_conv_b, lru_w_ra, lru_b_ra, lru_w_rx, lru_b_rx,
           lru_lambda, lru_w_out):
    batch, seq, _ = x.shape
    cos_t, sin_a, sin_b = _rope_tables(seq)
    bf = lambda a: a.astype(BF16)
    vecs = lambda a: a.reshape(-1, 1, a.shape[-1])
    w_qkv, w_o = (bf(w) for w in _permute_q_heads(attn_w_qkv, attn_w_o))
    lru = (bf(lru_w_in), lru_conv_w, vecs(lru_conv_b), bf(lru_w_ra), vecs(lru_b_ra),
           bf(lru_w_rx), vecs(lru_b_rx), vecs(lru_lambda), bf(lru_w_out))
    g, b = vecs(ln_g), vecs(ln_b)

    h = x.reshape(batch * seq, D_MODEL)
    wi, wo = bf(ffn1_w_in[0]), bf(ffn1_w_out[0])
    for i in range(DEPTH):
        h, wi, wo = _ffn_ln(h, wi, wo, g, b, 3 * i, next_w=(ffn2_w_in, ffn2_w_out, i))
        j = i // 2
        next_w = (ffn1_w_in, ffn1_w_out, i + 1) if i + 1 < DEPTH else None
        if i % 2 == 0:
            q, k, v = _qkv_rope(h, w_qkv, cos_t, sin_a, sin_b, seq, j)
            o = _swa_attn(attn_sinks[j], q, k, v, seq)
            outs = _ffn_ln(h, wi, wo, g, b, 3 * i + 2, next_w=next_w, proj=(o, w_o, j, 3 * i + 1))
        else:
            outs = _lru_ffn(h, lru, g, b, wi, wo, seq, j, 3 * i + 1, 3 * i + 2, next_w=next_w)
        h = outs[0]
        if next_w is not None:
            wi, wo = outs[1:]
    return h.reshape(batch, seq, D_MODEL)
```
